```python
import math
import jax, jax.numpy as jnp
from jax import lax
import numpy as np

D_MODEL = 2048
BATCH = 2
SEQ = 8192
DEPTH = 1
DEC_BATCH = 1
DEC_SEQ = 16384
PAST_LEN = 128

NORM_EPS = 1e-5
ATT_GROUPS = ((128, 1), (512, 4), (2048, 16))
N_ATT_GROUPS = 3
ATT_HEADS_PER_GROUP = 12
ATT_HEAD_DIM = 128
ATT_HEADS = N_ATT_GROUPS * ATT_HEADS_PER_GROUP
ATT_QKV_W = ATT_HEADS * ATT_HEAD_DIM
ATT_W = ATT_HEADS_PER_GROUP * ATT_HEAD_DIM
ALIBI_MAX_EXP = 8.0
SSM_EXPAND = 2
D_INNER = SSM_EXPAND * D_MODEL
SSM_HEAD_DIM = 64
SSM_HEADS = D_INNER // SSM_HEAD_DIM
SSM_GROUPS = 8
D_STATE = 128
D_CONV = 5
CHUNK = 128
CONV_CH = D_INNER + 2 * SSM_GROUPS * D_STATE
N_BRANCH = 2
IN_COLS = 3 * ATT_QKV_W + ATT_W + D_INNER + CONV_CH + 2 * SSM_HEADS + N_BRANCH * D_MODEL

kernel_name = 'hybrid_dilated_attn_ssd_encoder'


def rmsnorm(x, w):
    xf = x.astype(jnp.float32)
    y = xf * lax.rsqrt(jnp.mean(xf * xf, axis=-1, keepdims=True) + NORM_EPS)
    return (y * w.astype(jnp.float32)).astype(x.dtype)


def alibi_slopes(n_heads):
    return jnp.exp2(-ALIBI_MAX_EXP * jnp.arange(1, n_heads + 1, dtype=jnp.float32) / n_heads)


def dilated_window_attention(q, k, v, slopes, window, dilation):
    b, l, h, e = q.shape
    half = window // (2 * dilation)
    blk = half
    n = l // dilation
    nb = -(-n // blk)
    npad = nb * blk

    def to_sub(t):
        t = t.reshape(b, n, dilation, h, e).transpose(0, 2, 1, 3, 4)
        return jnp.pad(t, ((0, 0), (0, 0), (0, npad - n), (0, 0), (0, 0)))

    def key_windows(t):
        t = jnp.pad(t, ((0, 0), (0, 0), (blk, blk), (0, 0), (0, 0))).reshape(b, dilation, nb + 2, blk, h, e)
        return jnp.concatenate([t[:, :, :-2], t[:, :, 1:-1], t[:, :, 2:]], axis=3)

    qs = (to_sub(q) * (e ** -0.5)).reshape(b, dilation, nb, blk, h, e)
    ks = key_windows(to_sub(k))
    vs = key_windows(to_sub(v))
    qi = jnp.arange(nb)[:, None] * blk + jnp.arange(blk)[None, :]
    ki = (jnp.arange(nb)[:, None] - 1) * blk + jnp.arange(3 * blk)[None, :]
    rel = qi[:, :, None] - ki[:, None, :]
    valid = (jnp.abs(rel) <= half) & (ki[:, None, :] >= 0) & (ki[:, None, :] < n)
    dist = (jnp.abs(rel) * dilation).astype(jnp.float32)
    s = jnp.einsum('bdnqhe,bdnkhe->bdnhqk', qs, ks, preferred_element_type=jnp.float32)
    s = s - slopes[:, None, None] * dist[:, None]
    s = jnp.where(valid[:, None], s, -jnp.inf)
    lse = jax.nn.logsumexp(s, axis=-1)
    p = jnp.exp(s - lse[..., None]).astype(v.dtype)
    o = jnp.einsum('bdnhqk,bdnkhe->bdnqhe', p, vs)
    o = o.reshape(b, dilation, npad, h, e)[:, :, :n].transpose(0, 2, 1, 3, 4).reshape(b, l, h, e)
    lse = lse.transpose(0, 1, 2, 4, 3).reshape(b, dilation, npad, h)[:, :, :n]
    lse = lse.transpose(0, 2, 1, 3).reshape(b, l, h)
    return o, lse


def ssd_chunked(x, dt, a, bm, cm):
    b, l, h, p = x.shape
    g, n = bm.shape[2], bm.shape[3]
    r = h // g
    c = l // CHUNK
    f32 = jnp.float32
    xq = (x.astype(f32) * dt[..., None]).reshape(b, c, CHUNK, g, r, p)
    acs = jnp.cumsum((dt * a).reshape(b, c, CHUNK, g, r), axis=2)
    bq = bm.astype(f32).reshape(b, c, CHUNK, g, n)
    cq = cm.astype(f32).reshape(b, c, CHUNK, g, n)
    lower = jnp.tril(jnp.ones((CHUNK, CHUNK), dtype=bool))
    seg = acs[:, :, :, None] - acs[:, :, None, :]
    decay_ts = jnp.exp(jnp.where(lower[:, :, None, None], seg, -jnp.inf))
    cb = jnp.einsum('bctgn,bcsgn->bctsg', cq, bq)
    y_diag = jnp.einsum('bctsg,bctsgr,bcsgrp->bctgrp', cb, decay_ts, xq)
    decay_end = jnp.exp(acs[:, :, -1:] - acs)
    states = jnp.einsum('bcsgn,bcsgr,bcsgrp->bcgrpn', bq, decay_end, xq)
    chunk_decay = jnp.exp(acs[:, :, -1])

    def step(state, inp):
        dec, st = inp
        return state * dec[..., None, None] + st, state

    init = jnp.zeros((b, g, r, p, n), f32)
    _, prev = lax.scan(step, init, (jnp.moveaxis(chunk_decay, 1, 0), jnp.moveaxis(states, 1, 0)))
    prev = jnp.moveaxis(prev, 0, 1)
    y_off = jnp.einsum('bctgn,bcgrpn,bctgr->bctgrp', cq, prev, jnp.exp(acs))
    return (y_diag + y_off).reshape(b, l, h, p)


def bidirectional_ssd(xs, dt_raw, bm, cm, dt_bias, a_log):
    f32 = jnp.float32
    dt_f = jax.nn.softplus(dt_raw[:, :, 0].astype(f32) + dt_bias[0].astype(f32))
    dt_b = jax.nn.softplus(dt_raw[:, :, 1].astype(f32) + dt_bias[1].astype(f32))
    a_f = -jnp.exp(a_log[0].astype(f32))
    a_b = -jnp.exp(a_log[1].astype(f32))
    y_fwd = ssd_chunked(xs, dt_f, a_f, bm, cm)
    y_bwd = jnp.flip(ssd_chunked(jnp.flip(xs, 1), jnp.flip(dt_b, 1), a_b, jnp.flip(bm, 1), jnp.flip(cm, 1)), 1)
    return y_fwd + y_bwd


def hybrid_layer(x, norm_w, w_in, b_gate, conv_w, conv_b, dt_bias, a_log, d_skip, ssm_norm_w,
                 w_attn_o, w_ssm_o, w_out):
    b, l, _ = x.shape
    f32 = jnp.float32
    hn = rmsnorm(x, norm_w)
    proj = jnp.einsum('bld,df->blf', hn, w_in)
    sizes = [ATT_QKV_W, ATT_QKV_W, ATT_QKV_W, ATT_W, D_INNER, CONV_CH, 2 * SSM_HEADS]
    q, k, v, g_att, z, xbc, dt_raw, gate_logits = jnp.split(proj, np.cumsum(sizes).tolist(), axis=-1)

    hs = (b, l, N_ATT_GROUPS, ATT_HEADS_PER_GROUP, ATT_HEAD_DIM)
    q, k, v = q.reshape(hs), k.reshape(hs), v.reshape(hs)
    slopes = alibi_slopes(ATT_HEADS).reshape(N_ATT_GROUPS, ATT_HEADS_PER_GROUP)
    outs, lses = [], []
    for gi, (window, dilation) in enumerate(ATT_GROUPS):
        o, s = dilated_window_attention(q[:, :, gi], k[:, :, gi], v[:, :, gi], slopes[gi], window, dilation)
        outs.append(o)
        lses.append(s)
    mix = jax.nn.softmax(jnp.stack(lses, axis=0), axis=0)
    att = jnp.einsum('gblh,gblhe->blhe', mix, jnp.stack(outs, axis=0).astype(f32))
    att = att.reshape(b, l, ATT_W).astype(x.dtype) * jax.nn.silu(g_att)
    att = att @ w_attn_o

    xbc = lax.conv_general_dilated(xbc, conv_w[:, None, :], (1,), [(D_CONV // 2, D_CONV // 2)],
                                   dimension_numbers=('NWC', 'WIO', 'NWC'), feature_group_count=CONV_CH)
    xbc = jax.nn.silu(xbc + conv_b)
    xs, bm, cm = jnp.split(xbc, [D_INNER, D_INNER + SSM_GROUPS * D_STATE], axis=-1)
    xs = xs.reshape(b, l, SSM_HEADS, SSM_HEAD_DIM)
    bm = bm.reshape(b, l, SSM_GROUPS, D_STATE)
    cm = cm.reshape(b, l, SSM_GROUPS, D_STATE)
    y = bidirectional_ssd(xs, dt_raw.reshape(b, l, 2, SSM_HEADS), bm, cm, dt_bias, a_log)
    y = y + d_skip.astype(f32)[:, None] * xs.astype(f32)
    y = (y.reshape(b, l, D_INNER) * jax.nn.silu(z.astype(f32))).astype(x.dtype)
    ssm = rmsnorm(y, ssm_norm_w) @ w_ssm_o

    gates = jax.nn.sigmoid(gate_logits.reshape(b, l, N_BRANCH, D_MODEL) + b_gate)
    merged = gates[:, :, 0] * att + gates[:, :, 1] * ssm
    return x + merged @ w_out


def setup_inputs(seed: int = 0) -> dict:
    key = jax.random.key(seed)
    ks = jax.random.split(key, 16)
    f32 = jnp.float32
    L = DEPTH

    def nrm(k, shape, scale):
        return jax.random.normal(k, shape, f32) * scale

    dt0 = jnp.exp(jax.random.uniform(ks[7], (L, 2, SSM_HEADS), f32, math.log(1e-3), math.log(1e-1)))
    return {
        'x_prompt': nrm(ks[0], (BATCH, SEQ, D_MODEL), 1.0),
        'x_sample': nrm(ks[1], (DEC_BATCH, DEC_SEQ, D_MODEL), 1.0),
        'norm_w': 1.0 + nrm(ks[2], (L, D_MODEL), 0.01),
        'w_in': nrm(ks[3], (L, D_MODEL, IN_COLS), D_MODEL ** -0.5),
        'b_gate': nrm(ks[4], (L, N_BRANCH, D_MODEL), 0.01),
        'conv_w': nrm(ks[5], (L, D_CONV, CONV_CH), D_CONV ** -0.5),
        'conv_b': nrm(ks[6], (L, CONV_CH), 0.01),
        'dt_bias': dt0 + jnp.log(-jnp.expm1(-dt0)),
        'a_log': jnp.log(jax.random.uniform(ks[8], (L, 2, SSM_HEADS), f32, 1.0, 16.0)),
        'd_skip': 1.0 + nrm(ks[9], (L, SSM_HEADS), 0.01),
        'ssm_norm_w': 1.0 + nrm(ks[10], (L, D_INNER), 0.01),
        'w_attn_o': nrm(ks[11], (L, ATT_W, D_MODEL), ATT_W ** -0.5),
        'w_ssm_o': nrm(ks[12], (L, D_INNER, D_MODEL), D_INNER ** -0.5),
        'w_out': nrm(ks[13], (L, D_MODEL, D_MODEL), D_MODEL ** -0.5),
        'final_norm_w': 1.0 + nrm(ks[14], (D_MODEL,), 0.01),
    }


def reference(x_prompt, x_sample, norm_w, w_in, b_gate, conv_w, conv_b, dt_bias, a_log, d_skip,
              ssm_norm_w, w_attn_o, w_ssm_o, w_out, final_norm_w):
    def encode(x):
        for i in range(DEPTH):
            x = hybrid_layer(x, norm_w[i], w_in[i], b_gate[i], conv_w[i], conv_b[i], dt_bias[i], a_log[i],
                             d_skip[i], ssm_norm_w[i], w_attn_o[i], w_ssm_o[i], w_out[i])
        return rmsnorm(x, final_norm_w)

    y_prompt = encode(x_prompt)
    y_sample = encode(x_sample)
    return (y_prompt, y_sample)
```

```python
import functools
import math

import numpy as np
import jax
import jax.numpy as jnp
from jax import lax
from jax.experimental import pallas as pl
from jax.experimental.pallas import tpu as pltpu

F32 = jnp.float32
BF16 = jnp.bfloat16

NORM_EPS = 1e-5
ATT_GROUPS = ((128, 1), (512, 4), (2048, 16))
N_GROUPS = 3
HEADS_PER_GROUP = 12
HEAD_DIM = 128
ATT_HEADS = N_GROUPS * HEADS_PER_GROUP
ATT_W = HEADS_PER_GROUP * HEAD_DIM
QKV_W = ATT_HEADS * HEAD_DIM
ALIBI_MAX_EXP = 8.0
HALF = 64
SSM_HEAD_DIM = 64
SSM_GROUPS = 8
D_STATE = 128
D_CONV = 5
N_BRANCH = 2

VMEM_LIMIT_BYTES = 56 * 1024 * 1024
PERM_BLOCK = 256
ATT_BLOCK = 2048
SUB = 128
NEG_BIG = -1e30


def _cparams(*sem):
    return pltpu.CompilerParams(dimension_semantics=sem, vmem_limit_bytes=VMEM_LIMIT_BYTES)


def _perm_matrix(d):
    n = PERM_BLOCK // d
    p = np.zeros((PERM_BLOCK, PERM_BLOCK), np.float32)
    for r in range(d):
        for i in range(n):
            p[r * n + i, i * d + r] = 1.0
    return jnp.asarray(p, BF16)


def _prologue_kernel(x_ref, w_ref, p4_ref, p16_ref, nat_ref, o4_ref, o16_ref):
    x = x_ref[...]
    y = x * lax.rsqrt(jnp.mean(x * x, axis=-1, keepdims=True) + NORM_EPS)
    yb = (y * w_ref[...]).astype(BF16)
    nat_ref[...] = yb
    y4 = jnp.dot(p4_ref[...], yb, preferred_element_type=F32).astype(BF16)
    n4 = PERM_BLOCK // 4
    for r in range(4):
        o4_ref[r] = y4[r * n4:(r + 1) * n4]
    y16 = jnp.dot(p16_ref[...], yb, preferred_element_type=F32).astype(BF16)
    n16 = PERM_BLOCK // 16
    for r in range(16):
        o16_ref[r] = y16[r * n16:(r + 1) * n16]


def _prologue(x, norm_w):
    b, l, d = x.shape
    nblk = l // PERM_BLOCK
    return pl.pallas_call(
        _prologue_kernel,
        grid=(b, nblk),
        in_specs=[
            pl.BlockSpec((None, PERM_BLOCK, d), lambda bi, i: (bi, i, 0)),
            pl.BlockSpec((1, d), lambda bi, i: (0, 0)),
            pl.BlockSpec((PERM_BLOCK, PERM_BLOCK), lambda bi, i: (0, 0)),
            pl.BlockSpec((PERM_BLOCK, PERM_BLOCK), lambda bi, i: (0, 0)),
        ],
        out_specs=[
            pl.BlockSpec((None, PERM_BLOCK, d), lambda bi, i: (bi, i, 0)),
            pl.BlockSpec((None, 4, PERM_BLOCK // 4, d), lambda bi, i: (bi, 0, i, 0)),
            pl.BlockSpec((None, 16, PERM_BLOCK // 16, d), lambda bi, i: (bi, 0, i, 0)),
        ],
        out_shape=[
            jax.ShapeDtypeStruct((b, l, d), BF16),
            jax.ShapeDtypeStruct((b, 4, l // 4, d), BF16),
            jax.ShapeDtypeStruct((b, 16, l // 16, d), BF16),
        ],
        compiler_params=_cparams("parallel", "parallel"),
        name="prologue_norm_permute",
    )(x, norm_w.reshape(1, d), _perm_matrix(4), _perm_matrix(16))


def _proj_kernel(x_ref, w_ref, b_ref, o_ref, *, mode):
    acc = jnp.dot(x_ref[...], w_ref[...], preferred_element_type=F32)
    if mode == "qkv":
        scale = jnp.where(pl.program_id(0) == 0, HEAD_DIM ** -0.5, 1.0).astype(F32)
        acc = acc * scale
    elif mode == "silu":
        acc = acc * jax.nn.sigmoid(acc)
    elif mode == "softplus":
        acc = jax.nn.softplus(acc + b_ref[...])
    elif mode == "sigmoid":
        acc = jax.nn.sigmoid(acc + b_ref[...])
    o_ref[...] = acc.astype(o_ref.dtype)


def _proj(x, w, bias, *, mode, tn, out_dtype, tm=1024):
    m, k = x.shape
    n = w.shape[1]
    tm = min(tm, m)
    nj, ni = n // tn, m // tm
    if mode == "qkv":
        out_shape = jax.ShapeDtypeStruct((nj, m, tn), out_dtype)
        out_spec = pl.BlockSpec((None, tm, tn), lambda j, i: (j, i, 0))
    else:
        out_shape = jax.ShapeDtypeStruct((m, n), out_dtype)
        out_spec = pl.BlockSpec((tm, tn), lambda j, i: (i, j))
    return pl.pallas_call(
        functools.partial(_proj_kernel, mode=mode),
        grid=(nj, ni),
        in_specs=[
            pl.BlockSpec((tm, k), lambda j, i: (i, 0)),
            pl.BlockSpec((k, tn), lambda j, i: (0, j)),
            pl.BlockSpec((1, tn), lambda j, i: (0, j)),
        ],
        out_specs=out_spec,
        out_shape=out_shape,
        compiler_params=_cparams("parallel", "parallel"),
        name="in_proj_" + mode,
    )(x, w, bias)


def _attn_kernel(sl_ref, rel_ref,
                 q0, k0p, k0c, k0n, v0p, v0c, v0n,
                 q1, k1p, k1c, k1n, v1p, v1c, v1n,
                 q2, k2p, k2c, k2n, v2p, v2c, v2n,
                 gate_ref, o_ref,
                 kf0, vf0, kf1, vf1, kf2, vf2, acc_s, mb_s, lb_s):
    i = pl.program_id(1)
    h = pl.program_id(2)
    is_first = i == 0
    is_last = i == pl.num_programs(1) - 1

    def fill(kf, vf, kp, kc, kn, vp, vc, vn, own):
        lead = (slice(None),) * (len(kf.shape) - 2)
        for lo, hi, ks, vs in ((0, HALF, kp, vp), (HALF, HALF + own, kc, vc), (HALF + own, 2 * HALF + own, kn, vn)):
            kf[lead + (slice(lo, hi), slice(None))] = ks[...]
            vf[lead + (slice(lo, hi), slice(0, HEAD_DIM))] = vs[...]
            vf[lead + (slice(lo, hi), slice(HEAD_DIM, 2 * HEAD_DIM))] = jnp.ones(vs.shape, BF16)

    fill(kf0, vf0, k0p, k0c, k0n, v0p, v0c, v0n, ATT_BLOCK)
    fill(kf1, vf1, k1p, k1c, k1n, v1p, v1c, v1n, ATT_BLOCK // 4)
    fill(kf2, vf2, k2p, k2c, k2n, v2p, v2c, v2n, ATT_BLOCK // 16)

    absrel = rel_ref[...]
    col = lax.broadcasted_iota(jnp.int32, (SUB, 2 * SUB), 1)
    in_window = absrel <= float(HALF)
    mask_lo = jnp.logical_and(is_first, col < HALF)
    mask_hi = jnp.logical_and(is_last, col >= HALF + SUB)

    def bias(g):
        dist = absrel * float(ATT_GROUPS[g][1])
        return jnp.where(in_window, -(sl_ref[g, h] * dist), NEG_BIG)

    def attend(g, q, k, v1, nb, rows):
        s = lax.dot_general(q, k, (((1,), (1,)), ((), ())), preferred_element_type=F32) + nb
        m = jnp.max(s, axis=-1, keepdims=True)
        p = jnp.exp(s - m).astype(BF16)
        al = jnp.dot(p, v1, preferred_element_type=F32)
        acc_s[g, rows, :] = al[:, :HEAD_DIM]
        lb_s[g, rows, :] = al[:, HEAD_DIM:]
        mb_s[g, rows, :] = jnp.broadcast_to(m, (SUB, HEAD_DIM))

    nb = bias(0)
    nsub = ATT_BLOCK // SUB

    nb_lo = jnp.where(mask_lo, NEG_BIG, nb)
    nb_hi = jnp.where(mask_hi, NEG_BIG, nb)
    for j in range(nsub):
        nbj = nb_lo if j == 0 else (nb_hi if j == nsub - 1 else nb)
        attend(0, q0[pl.ds(j * SUB, SUB), :], kf0[pl.ds(j * SUB, 2 * SUB), :], vf0[pl.ds(j * SUB, 2 * SUB), :],
               nbj, pl.ds(j * SUB, SUB))

    nb = bias(1)
    nb_lo = jnp.where(mask_lo, NEG_BIG, nb)
    nb_hi = jnp.where(mask_hi, NEG_BIG, nb)
    nsub1 = ATT_BLOCK // 4 // SUB

    def g1_res(r, c):
        for j in range(nsub1):
            nbj = nb_lo if j == 0 else (nb_hi if j == nsub1 - 1 else nb)
            attend(1, q1[r, pl.ds(j * SUB, SUB), :], kf1[r, pl.ds(j * SUB, 2 * SUB), :],
                   vf1[r, pl.ds(j * SUB, 2 * SUB), :], nbj, pl.ds(j * SUB * 4 + r, SUB, stride=4))
        return c

    lax.fori_loop(0, 4, g1_res, 0)

    nb = jnp.where(jnp.logical_or(mask_lo, mask_hi), NEG_BIG, bias(2))

    def g2_res(it, c):
        for rr in range(4):
            r = it * 4 + rr
            attend(2, q2[r], kf2[r], vf2[r], nb, pl.ds(r, SUB, stride=16))
        return c

    lax.fori_loop(0, 4, g2_res, 0)

    chunk = 256

    def merge(cix, c):
        rows = pl.ds(pl.multiple_of(cix * chunk, chunk), chunk)
        m0, m1, m2 = mb_s[0, rows, :], mb_s[1, rows, :], mb_s[2, rows, :]
        mm = jnp.maximum(jnp.maximum(m0, m1), m2)
        w0, w1, w2 = jnp.exp(m0 - mm), jnp.exp(m1 - mm), jnp.exp(m2 - mm)
        num = w0 * acc_s[0, rows, :] + w1 * acc_s[1, rows, :] + w2 * acc_s[2, rows, :]
        den = w0 * lb_s[0, rows, :] + w1 * lb_s[1, rows, :] + w2 * lb_s[2, rows, :]
        o_ref[rows, :] = (num / den * gate_ref[rows, :]).astype(o_ref.dtype)
        return c

    lax.fori_loop(0, ATT_BLOCK // chunk, merge, 0)


def _alibi_table():
    hh = np.arange(1, ATT_HEADS + 1, dtype=np.float32)
    return jnp.exp2(-ALIBI_MAX_EXP * jnp.asarray(hh) / ATT_HEADS).reshape(N_GROUPS, HEADS_PER_GROUP)


def _rel_table():
    qi = np.arange(SUB)[:, None]
    kj = np.arange(2 * SUB)[None, :] - HALF
    return jnp.asarray(np.abs(qi - kj), F32)


def _attention(qkv0, qkv1, qkv2, gate, b, l):
    nblk = l // ATT_BLOCK
    qkv0 = qkv0.reshape(3, b, l, ATT_W)
    qkv1 = qkv1.reshape(3, b, 4, l // 4, ATT_W)
    qkv2 = qkv2.reshape(3, b, 16, l // 16, ATT_W)
    gate = gate.reshape(b, l, ATT_W)

    def specs(d):
        own = ATT_BLOCK // d
        hb = own // HALF
        nh = l // d // HALF
        lead = (None, None) if d == 1 else (None, None, d)
        zero = () if d == 1 else (0,)

        def mk(rows, which, fn):
            return pl.BlockSpec(lead + (rows, HEAD_DIM), lambda bi, i, h: (which, bi) + zero + (fn(i), h))

        cur = lambda i: i
        prv = lambda i: jnp.maximum(i * hb - 1, 0)
        nxt = lambda i: jnp.minimum((i + 1) * hb, nh - 1)
        return [mk(own, 0, cur),
                mk(HALF, 1, prv), mk(own, 1, cur), mk(HALF, 1, nxt),
                mk(HALF, 2, prv), mk(own, 2, cur), mk(HALF, 2, nxt)]

    def scr(d):
        own = ATT_BLOCK // d
        lead = () if d == 1 else (d,)
        return [pltpu.VMEM(lead + (own + 2 * HALF, HEAD_DIM), BF16),
                pltpu.VMEM(lead + (own + 2 * HALF, 2 * HEAD_DIM), BF16)]

    in_specs = [pl.BlockSpec(memory_space=pltpu.SMEM),
                pl.BlockSpec((SUB, 2 * SUB), lambda bi, i, h: (0, 0))]
    in_specs += specs(1) + specs(4) + specs(16)
    in_specs += [pl.BlockSpec((None, ATT_BLOCK, HEAD_DIM), lambda bi, i, h: (bi, i, h))]
    return pl.pallas_call(
        _attn_kernel,
        grid=(b, nblk, HEADS_PER_GROUP),
        in_specs=in_specs,
        out_specs=pl.BlockSpec((None, ATT_BLOCK, HEAD_DIM), lambda bi, i, h: (bi, i, h)),
        out_shape=jax.ShapeDtypeStruct((b, l, ATT_W), BF16),
        scratch_shapes=scr(1) + scr(4) + scr(16) + [pltpu.VMEM((N_GROUPS, ATT_BLOCK, HEAD_DIM), F32)] * 3,
        compiler_params=_cparams("parallel", "parallel", "arbitrary"),
        name="dilated_attention",
    )(_alibi_table(), _rel_table(), *([qkv0] * 7), *([qkv1] * 7), *([qkv2] * 7), gate)


CONV_ROWS = 512
CONV_COLS = 1024
CONV_HALO = 8


def _conv_kernel(xp_ref, xc_ref, xn_ref, w_ref, b_ref, o_ref):
    i = pl.program_id(1)
    prev = jnp.where(i == 0, 0.0, xp_ref[...])
    nxt = jnp.where(i == pl.num_programs(1) - 1, 0.0, xn_ref[...])
    ext = jnp.concatenate([prev, xc_ref[...], nxt], axis=0)
    w = w_ref[...]
    acc = b_ref[...]
    for k in range(D_CONV):
        lo = CONV_HALO + k - D_CONV // 2
        acc = acc + ext[lo:lo + CONV_ROWS] * w[k:k + 1]
    o_ref[...] = (acc * jax.nn.sigmoid(acc)).astype(o_ref.dtype)


def _conv(xbc, conv_w, conv_b, b, l, c0, c1, out_dtype):
    nb = l // CONV_ROWS
    hb = CONV_ROWS // CONV_HALO
    j0 = c0 // CONV_COLS
    nj = (c1 - c0) // CONV_COLS
    return pl.pallas_call(
        _conv_kernel,
        grid=(b, nb, nj),
        in_specs=[
            pl.BlockSpec((None, CONV_HALO, CONV_COLS), lambda bi, i, j: (bi, jnp.maximum(i * hb - 1, 0), j0 + j)),
            pl.BlockSpec((None, CONV_ROWS, CONV_COLS), lambda bi, i, j: (bi, i, j0 + j)),
            pl.BlockSpec((None, CONV_HALO, CONV_COLS),
                         lambda bi, i, j: (bi, jnp.minimum((i + 1) * hb, l // CONV_HALO - 1), j0 + j)),
            pl.BlockSpec((D_CONV, CONV_COLS), lambda bi, i, j: (0, j0 + j)),
            pl.BlockSpec((1, CONV_COLS), lambda bi, i, j: (0, j0 + j)),
        ],
        out_specs=pl.BlockSpec((None, CONV_ROWS, CONV_COLS), lambda bi, i, j: (bi, i, j)),
        out_shape=jax.ShapeDtypeStruct((b, l, c1 - c0), out_dtype),
        compiler_params=_cparams("parallel", "parallel", "parallel"),
        name="conv_silu",
    )(xbc, xbc, xbc, conv_w, conv_b)


SSD_Q = 128
SSD_BLOCK = 256


def _prefix_sum_rows(x):
    n = x.shape[0]
    row = lax.broadcasted_iota(jnp.int32, x.shape, 0)
    k = 1
    while k < n:
        x = x + jnp.where(row >= k, pltpu.roll(x, k, axis=0), 0.0)
        k *= 2
    return x


def _ssd_kernel(x_ref, b_ref, c_ref, dt_ref, a_ref, e_ref, y_ref, st_ref, *, heads, groups):
    q = SSD_Q
    p = SSM_HEAD_DIM
    n = D_STATE
    hpg = heads // groups
    d = pl.program_id(1)
    fwd = d == 0
    nc = x_ref.shape[0] // q

    @pl.when(pl.program_id(2) == 0)
    def _():
        st_ref[...] = jnp.zeros(st_ref.shape, F32)

    row = lax.broadcasted_iota(jnp.int32, (q, q), 0)
    col = lax.broadcasted_iota(jnp.int32, (q, q), 1)
    tri = (row - col) * jnp.where(fwd, 1, -1) >= 0
    lane = lax.broadcasted_iota(jnp.int32, (q, 2 * heads), 1)

    def expand(v):
        hi = v.astype(BF16)
        lo = (v - hi.astype(F32)).astype(BF16)
        return jnp.dot(jnp.concatenate([hi, lo], axis=1), e_ref[...], preferred_element_type=F32)

    def chunk(c, carry):
        cc = jnp.where(fwd, c, nc - 1 - c)
        rows = pl.ds(pl.multiple_of(cc * q, q), q)
        dt = dt_ref[rows, :]
        da = dt * a_ref[...]
        pre = _prefix_sum_rows(da)
        tot = pre[q - 1:q, :]
        acs_all = jnp.where(lane < heads, pre, tot - pre + da)
        dtx = expand(dt)
        wx = expand(dt * jnp.exp(tot - acs_all))
        sc = expand(jnp.exp(acs_all))
        cdx = jnp.where(fwd, sc[q - 1:q, :], sc[0:1, :])
        xs = x_ref[rows, :]
        xq = (xs * dtx).astype(BF16)
        xdec = (xs * wx).astype(BF16)
        acs = jnp.where(fwd, acs_all, pltpu.roll(acs_all, heads, axis=1))
        acs_t = acs.T
        for g in range(groups):
            bg = b_ref[rows, g * n:(g + 1) * n]
            cg = c_ref[rows, g * n:(g + 1) * n]
            cb = lax.dot_general(cg, bg, (((1,), (1,)), ((), ())), preferred_element_type=F32)
            outs = []
            for hh in range(0, hpg, 2):
                ls = []
                for h in (g * hpg + hh, g * hpg + hh + 1):
                    seg = acs[:, h:h + 1] - acs_t[h:h + 1, :]
                    ls.append((jnp.where(tri, jnp.exp(seg), 0.0) * cb).astype(BF16))
                h0 = g * hpg + hh
                pair = xq[:, h0 * p:(h0 + 2) * p]
                lane2 = lax.broadcasted_iota(jnp.int32, pair.shape, 1)
                zero = jnp.zeros_like(pair)
                rhs = jnp.concatenate([jnp.where(lane2 < p, pair, zero), jnp.where(lane2 >= p, pair, zero)], axis=0)
                outs.append(jnp.dot(jnp.concatenate(ls, axis=1), rhs, preferred_element_type=F32))
            cols = slice(g * hpg * p, (g + 1) * hpg * p)
            st = st_ref[g]
            y_off = jnp.dot(cg, st.astype(BF16), preferred_element_type=F32) * sc[:, cols]
            y_ref[rows, cols] = jnp.concatenate(outs, axis=1) + y_off
            new = lax.dot_general(bg, xdec[:, cols], (((0,), (0,)), ((), ())), preferred_element_type=F32)
            st_ref[g] = st * cdx[:, cols] + new
        return carry

    lax.fori_loop(0, nc, chunk, 0)


def _ssd(xs, bm, cm, dt, a_row, b, l):
    hp = xs.shape[-1]
    heads = hp // SSM_HEAD_DIM
    nblk = l // SSD_BLOCK
    e = np.zeros((2, 4 * heads, hp), np.float32)
    for dd in range(2):
        for h in range(heads):
            e[dd, dd * heads + h, h * SSM_HEAD_DIM:(h + 1) * SSM_HEAD_DIM] = 1.0
            e[dd, 2 * heads + dd * heads + h, h * SSM_HEAD_DIM:(h + 1) * SSM_HEAD_DIM] = 1.0
    blk = lambda bi, dd, i: (bi, jnp.where(dd == 0, i, nblk - 1 - i), 0)
    return pl.pallas_call(
        functools.partial(_ssd_kernel, heads=heads, groups=SSM_GROUPS),
        grid=(b, 2, nblk),
        in_specs=[
            pl.BlockSpec((None, SSD_BLOCK, hp), blk),
            pl.BlockSpec((None, SSD_BLOCK, SSM_GROUPS * D_STATE), blk),
            pl.BlockSpec((None, SSD_BLOCK, SSM_GROUPS * D_STATE), blk),
            pl.BlockSpec((None, SSD_BLOCK, 2 * heads), blk),
            pl.BlockSpec((1, 2 * heads), lambda bi, dd, i: (0, 0)),
            pl.BlockSpec((None, 4 * heads, hp), lambda bi, dd, i: (dd, 0, 0)),
        ],
        out_specs=pl.BlockSpec((None, None, SSD_BLOCK, hp),
                               lambda bi, dd, i: (dd, bi, jnp.where(dd == 0, i, nblk - 1 - i), 0)),
        out_shape=jax.ShapeDtypeStruct((2, b, l, hp), F32),
        scratch_shapes=[pltpu.VMEM((SSM_GROUPS, D_STATE, hp // SSM_GROUPS), F32)],
        compiler_params=_cparams("parallel", "arbitrary", "arbitrary"),
        name="ssd_scan",
    )(xs, bm, cm, dt, a_row, jnp.asarray(e, BF16))


def _ssd_post_kernel(yf_ref, yb_ref, xs_ref, zs_ref, d_ref, w_ref, o_ref):
    y = yf_ref[...] + yb_ref[...] + d_ref[...] * xs_ref[...]
    y = y * zs_ref[...]
    y = y * lax.rsqrt(jnp.mean(y * y, axis=-1, keepdims=True) + NORM_EPS)
    o_ref[...] = (y * w_ref[...]).astype(o_ref.dtype)


def _ssd_post(y2, xs, zs, d_row, w_row, tm=256):
    m, n = xs.shape
    row = pl.BlockSpec((tm, n), lambda i: (i, 0))
    vec = pl.BlockSpec((1, n), lambda i: (0, 0))
    return pl.pallas_call(
        _ssd_post_kernel,
        grid=(m // tm,),
        in_specs=[pl.BlockSpec((None, tm, n), lambda i: (0, i, 0)), pl.BlockSpec((None, tm, n), lambda i: (1, i, 0)),
                  row, row, vec, vec],
        out_specs=row,
        out_shape=jax.ShapeDtypeStruct((m, n), BF16),
        compiler_params=_cparams("parallel"),
        name="ssd_gate_norm",
    )(y2, y2, xs, zs, d_row, w_row)


def _merge_kernel(att_ref, ssm_ref, wa_ref, ws_ref, ga_ref, gs_ref, o_ref):
    a = jnp.dot(att_ref[...], wa_ref[...], preferred_element_type=F32)
    s = jnp.dot(ssm_ref[...], ws_ref[...], preferred_element_type=F32)
    o_ref[...] = (ga_ref[...] * a + gs_ref[...] * s).astype(o_ref.dtype)


def _merge(att, ssm_in, wa, ws, gates, tm=512, tn=1024):
    m = att.shape[0]
    n = wa.shape[1]
    nj = n // tn
    return pl.pallas_call(
        _merge_kernel,
        grid=(nj, m // tm),
        in_specs=[
            pl.BlockSpec((tm, att.shape[1]), lambda j, i: (i, 0)),
            pl.BlockSpec((tm, ssm_in.shape[1]), lambda j, i: (i, 0)),
            pl.BlockSpec((wa.shape[0], tn), lambda j, i: (0, j)),
            pl.BlockSpec((ws.shape[0], tn), lambda j, i: (0, j)),
            pl.BlockSpec((tm, tn), lambda j, i: (i, j)),
            pl.BlockSpec((tm, tn), lambda j, i: (i, nj + j)),
        ],
        out_specs=pl.BlockSpec((tm, tn), lambda j, i: (i, j)),
        out_shape=jax.ShapeDtypeStruct((m, n), BF16),
        compiler_params=_cparams("parallel", "parallel"),
        name="branch_merge",
    )(att, ssm_in, wa, ws, gates, gates)


def _out_kernel(m_ref, w_ref, x_ref, nw_ref, o_ref):
    y = x_ref[...] + jnp.dot(m_ref[...], w_ref[...], preferred_element_type=F32)
    y = y * lax.rsqrt(jnp.mean(y * y, axis=-1, keepdims=True) + NORM_EPS)
    o_ref[...] = y * nw_ref[...]


def _out_proj(merged, w, x, nw_row, tm=512):
    m, n = x.shape
    return pl.pallas_call(
        _out_kernel,
        grid=(m // tm,),
        in_specs=[
            pl.BlockSpec((tm, merged.shape[1]), lambda i: (i, 0)),
            pl.BlockSpec(w.shape, lambda i: (0, 0)),
            pl.BlockSpec((tm, n), lambda i: (i, 0)),
            pl.BlockSpec((1, n), lambda i: (0, 0)),
        ],
        out_specs=pl.BlockSpec((tm, n), lambda i: (i, 0)),
        out_shape=jax.ShapeDtypeStruct((m, n), F32),
        compiler_params=_cparams("parallel"),
        name="out_proj_norm",
    )(merged, w, x, nw_row)


def kernel(x_prompt, x_sample, norm_w, w_in, b_gate, conv_w, conv_b, dt_bias, a_log, d_skip, ssm_norm_w, w_attn_o, w_ssm_o, w_out, final_norm_w):
    d_model = x_prompt.shape[-1]
    d_inner = ssm_norm_w.shape[-1]
    conv_ch = conv_w.shape[-1]
    ssm_heads = d_skip.shape[-1]
    w = w_in[0]
    offs = np.cumsum([0, QKV_W, QKV_W, QKV_W, ATT_W, d_inner, conv_ch, 2 * ssm_heads, N_BRANCH * d_model])
    seg = lambda a, b: w[:, a:b].astype(BF16)
    wq, wk, wv = (w[:, offs[i]:offs[i + 1]] for i in range(3))
    w_qkv = [jnp.concatenate([wq[:, g * ATT_W:(g + 1) * ATT_W], wk[:, g * ATT_W:(g + 1) * ATT_W],
                              wv[:, g * ATT_W:(g + 1) * ATT_W]], axis=1).astype(BF16) for g in range(N_GROUPS)]
    w_gatt, w_z, w_xbc, w_dt, w_gates = (seg(offs[i], offs[i + 1]) for i in range(3, 8))
    zero_b = jnp.zeros((1, N_BRANCH * d_model + conv_ch), F32)
    wa, ws, wo = w_attn_o[0].astype(BF16), w_ssm_o[0].astype(BF16), w_out[0].astype(BF16)
    a_row = (-jnp.exp(a_log[0].astype(F32))).reshape(1, -1)
    d_row = jnp.repeat(d_skip[0].astype(F32), SSM_HEAD_DIM).reshape(1, -1)
    conv_b = conv_b[0].reshape(1, -1)

    outs = []
    for x in (x_prompt, x_sample):
        b, l, _ = x.shape
        hn, hn4, hn16 = _prologue(x, norm_w[0])
        hn = hn.reshape(b * l, d_model)
        qkv0 = _proj(hn, w_qkv[0], zero_b, mode="qkv", tn=ATT_W, out_dtype=BF16)
        qkv1 = _proj(hn4.reshape(b * l, d_model), w_qkv[1], zero_b, mode="qkv", tn=ATT_W, out_dtype=BF16)
        qkv2 = _proj(hn16.reshape(b * l, d_model), w_qkv[2], zero_b, mode="qkv", tn=ATT_W, out_dtype=BF16)
        gatt = _proj(hn, w_gatt, zero_b, mode="silu", tn=ATT_W, out_dtype=F32)
        zs = _proj(hn, w_z, zero_b, mode="silu", tn=1024, out_dtype=F32)
        xbc = _proj(hn, w_xbc, zero_b, mode="raw", tn=1024, out_dtype=F32)
        dt = _proj(hn, w_dt, dt_bias[0].reshape(1, -1), mode="softplus", tn=2 * ssm_heads, out_dtype=F32)
        gates = _proj(hn, w_gates, b_gate[0].reshape(1, -1), mode="sigmoid", tn=1024, out_dtype=F32)
        att = _attention(qkv0, qkv1, qkv2, gatt, b, l).reshape(b * l, ATT_W)
        xbc = xbc.reshape(b, l, conv_ch)
        xs = _conv(xbc, conv_w[0], conv_b, b, l, 0, d_inner, F32)
        bc = _conv(xbc, conv_w[0], conv_b, b, l, d_inner, conv_ch, BF16)
        gn = SSM_GROUPS * D_STATE
        y2 = _ssd(xs, bc[..., :gn], bc[..., gn:], dt.reshape(b, l, 2 * ssm_heads), a_row, b, l)
        ssm_in = _ssd_post(y2.reshape(2, b * l, d_inner), xs.reshape(b * l, d_inner), zs, d_row,
                           ssm_norm_w[0].reshape(1, -1))
        merged = _merge(att, ssm_in, wa, ws, gates)
        out = _out_proj(merged, wo, x.reshape(b * l, d_model), final_norm_w.reshape(1, -1))
        outs.append(out.reshape(b, l, d_model))
    return tuple(outs)
```

```python
import functools
import math

import numpy as np
import jax
import jax.numpy as jnp
from jax import lax
from jax.experimental import pallas as pl
from jax.experimental.pallas import tpu as pltpu

F32 = jnp.float32
BF16 = jnp.bfloat16

NORM_EPS = 1e-5
ATT_GROUPS = ((128, 1), (512, 4), (2048, 16))
N_GROUPS = 3
HEADS_PER_GROUP = 12
HEAD_DIM = 128
ATT_HEADS = N_GROUPS * HEADS_PER_GROUP
ATT_W = HEADS_PER_GROUP * HEAD_DIM
QKV_W = ATT_HEADS * HEAD_DIM
ALIBI_MAX_EXP = 8.0
HALF = 64
SSM_HEAD_DIM = 64
SSM_GROUPS = 8
D_STATE = 128
D_CONV = 5
N_BRANCH = 2

VMEM_LIMIT_BYTES = 56 * 1024 * 1024
LANES = 128
PERM_BLOCK = 256
ATT_BLOCK = 2048
SUB = 128
NEG_BIG = -1e30


def _cparams(*sem):
    return pltpu.CompilerParams(dimension_semantics=sem, vmem_limit_bytes=VMEM_LIMIT_BYTES)


def _perm_matrix(d):
    n = PERM_BLOCK // d
    p = np.zeros((PERM_BLOCK, PERM_BLOCK), np.float32)
    for r in range(d):
        for i in range(n):
            p[r * n + i, i * d + r] = 1.0
    return jnp.asarray(p, BF16)


def _prologue_kernel(x_ref, w_ref, p4_ref, p16_ref, nat_ref, o4_ref, o16_ref):
    x = x_ref[...]
    y = x * lax.rsqrt(jnp.mean(x * x, axis=-1, keepdims=True) + NORM_EPS)
    yb = (y * w_ref[...]).astype(BF16)
    nat_ref[...] = yb
    y4 = jnp.dot(p4_ref[...], yb, preferred_element_type=F32).astype(BF16)
    n4 = PERM_BLOCK // 4
    for r in range(4):
        o4_ref[r] = y4[r * n4:(r + 1) * n4]
    y16 = jnp.dot(p16_ref[...], yb, preferred_element_type=F32).astype(BF16)
    n16 = PERM_BLOCK // 16
    for r in range(16):
        o16_ref[r] = y16[r * n16:(r + 1) * n16]


def _prologue(x, norm_w):
    b, l, d = x.shape
    nblk = l // PERM_BLOCK
    return pl.pallas_call(
        _prologue_kernel,
        grid=(b, nblk),
        in_specs=[
            pl.BlockSpec((None, PERM_BLOCK, d), lambda bi, i: (bi, i, 0)),
            pl.BlockSpec((1, d), lambda bi, i: (0, 0)),
            pl.BlockSpec((PERM_BLOCK, PERM_BLOCK), lambda bi, i: (0, 0)),
            pl.BlockSpec((PERM_BLOCK, PERM_BLOCK), lambda bi, i: (0, 0)),
        ],
        out_specs=[
            pl.BlockSpec((None, PERM_BLOCK, d), lambda bi, i: (bi, i, 0)),
            pl.BlockSpec((None, 4, PERM_BLOCK // 4, d), lambda bi, i: (bi, 0, i, 0)),
            pl.BlockSpec((None, 16, PERM_BLOCK // 16, d), lambda bi, i: (bi, 0, i, 0)),
        ],
        out_shape=[
            jax.ShapeDtypeStruct((b, l, d), BF16),
            jax.ShapeDtypeStruct((b, 4, l // 4, d), BF16),
            jax.ShapeDtypeStruct((b, 16, l // 16, d), BF16),
        ],
        compiler_params=_cparams("parallel", "parallel"),
        name="prologue_norm_permute",
    )(x, norm_w.reshape(1, d), _perm_matrix(4), _perm_matrix(16))


def _proj_kernel(x_ref, w_ref, b_ref, o_ref, *, mode):
    acc = jnp.dot(x_ref[...], w_ref[...], preferred_element_type=F32)
    if mode == "qkv":
        scale = jnp.where(pl.program_id(0) == 0, HEAD_DIM ** -0.5, 1.0).astype(F32)
        acc = acc * scale
    elif mode == "silu":
        acc = acc * jax.nn.sigmoid(acc)
    elif mode == "softplus":
        acc = jax.nn.softplus(acc + b_ref[...])
    elif mode == "sigmoid":
        acc = jax.nn.sigmoid(acc + b_ref[...])
    o_ref[...] = acc.astype(o_ref.dtype)


def _proj(x, w, bias, *, mode, tn, n, out_dtype, col0=0, col_step=1, tm=1024):
    m, k = x.shape
    tm = min(tm, m)
    nj, ni = n // tn, m // tm
    if mode == "qkv":
        out_shape = jax.ShapeDtypeStruct((nj, m, tn), out_dtype)
        out_spec = pl.BlockSpec((None, tm, tn), lambda j, i: (j, i, 0))
    else:
        out_shape = jax.ShapeDtypeStruct((m, n), out_dtype)
        out_spec = pl.BlockSpec((tm, tn), lambda j, i: (i, j))
    return pl.pallas_call(
        functools.partial(_proj_kernel, mode=mode),
        grid=(nj, ni),
        in_specs=[
            pl.BlockSpec((tm, k), lambda j, i: (i, 0)),
            pl.BlockSpec((k, tn), lambda j, i: (0, col0 + col_step * j)),
            pl.BlockSpec((1, tn), lambda j, i: (0, j)),
        ],
        out_specs=out_spec,
        out_shape=out_shape,
        compiler_params=_cparams("parallel", "parallel"),
        name="in_proj_" + mode,
    )(x, w, bias)


def _attn_kernel(sl_ref, rel_ref,
                 q0, k0p, k0c, k0n, v0p, v0c, v0n,
                 q1, k1p, k1c, k1n, v1p, v1c, v1n,
                 q2, k2p, k2c, k2n, v2p, v2c, v2n,
                 gate_ref, o_ref, acc_s, mb_s, lb_s):
    i = pl.program_id(1)
    h = pl.program_id(2)
    is_first = i == 0
    is_last = i == pl.num_programs(1) - 1
    ones = jnp.ones((2 * SUB, HEAD_DIM), BF16)

    def window(p_ref, c_ref, n_ref, lead, j, nsub):
        lo, hi = j * SUB - HALF, (j + 1) * SUB + HALF
        parts = []
        if j == 0:
            parts.append(p_ref[lead + (slice(None), slice(None))])
            lo = 0
        own_hi = min(hi, nsub * SUB)
        parts.append(c_ref[lead + (pl.ds(lo, own_hi - lo), slice(None))])
        if j == nsub - 1:
            parts.append(n_ref[lead + (slice(None), slice(None))])
        return parts[0] if len(parts) == 1 else jnp.concatenate(parts, axis=0)

    absrel = rel_ref[...]
    col = lax.broadcasted_iota(jnp.int32, (SUB, 2 * SUB), 1)
    in_window = absrel <= float(HALF)
    mask_lo = jnp.logical_and(is_first, col < HALF)
    mask_hi = jnp.logical_and(is_last, col >= HALF + SUB)

    def bias(g):
        dist = absrel * float(ATT_GROUPS[g][1])
        return jnp.where(in_window, -(sl_ref[g, h] * dist), NEG_BIG)

    def attend(g, q, k, v, nb, rows):
        s = lax.dot_general(q, k, (((1,), (1,)), ((), ())), preferred_element_type=F32) + nb
        m = jnp.max(s, axis=-1, keepdims=True)
        p = jnp.exp(s - m).astype(BF16)
        al = jnp.dot(p, jnp.concatenate([v, ones], axis=1), preferred_element_type=F32)
        acc_s[g, rows, :] = al[:, :HEAD_DIM]
        lb_s[g, rows, :] = al[:, HEAD_DIM:]
        mb_s[g, rows, :] = jnp.broadcast_to(m, (SUB, HEAD_DIM))

    def group(g, q_ref, kp, kc, kn, vp, vc, vn, lead, r, nsub):
        d = ATT_GROUPS[g][1]
        nb = nbs[g]
        for j in range(nsub):
            nbj = nb
            if j == 0:
                nbj = jnp.where(mask_lo, NEG_BIG, nbj)
            if j == nsub - 1:
                nbj = jnp.where(mask_hi, NEG_BIG, nbj)
            q = q_ref[lead + (pl.ds(j * SUB, SUB), slice(None))]
            rows = pl.ds(j * SUB, SUB) if d == 1 else pl.ds(j * SUB * d + r, SUB, stride=d)
            attend(g, q, window(kp, kc, kn, lead, j, nsub), window(vp, vc, vn, lead, j, nsub), nbj, rows)

    nbs = [bias(g) for g in range(N_GROUPS)]
    group(0, q0, k0p, k0c, k0n, v0p, v0c, v0n, (), 0, ATT_BLOCK // SUB)

    def g1_body(it, c):
        for rr in range(2):
            r = it * 2 + rr
            group(1, q1, k1p, k1c, k1n, v1p, v1c, v1n, (r,), r, ATT_BLOCK // 4 // SUB)
        return c

    lax.fori_loop(0, 2, g1_body, 0)

    def g2_body(it, c):
        for rr in range(8):
            r = it * 8 + rr
            group(2, q2, k2p, k2c, k2n, v2p, v2c, v2n, (r,), r, 1)
        return c

    lax.fori_loop(0, 2, g2_body, 0)

    chunk = 256

    def merge(cix, c):
        rows = pl.ds(pl.multiple_of(cix * chunk, chunk), chunk)
        m0, m1, m2 = mb_s[0, rows, :], mb_s[1, rows, :], mb_s[2, rows, :]
        mm = jnp.maximum(jnp.maximum(m0, m1), m2)
        w0, w1, w2 = jnp.exp(m0 - mm), jnp.exp(m1 - mm), jnp.exp(m2 - mm)
        num = w0 * acc_s[0, rows, :] + w1 * acc_s[1, rows, :] + w2 * acc_s[2, rows, :]
        den = w0 * lb_s[0, rows, :] + w1 * lb_s[1, rows, :] + w2 * lb_s[2, rows, :]
        o_ref[rows, :] = (num / den * gate_ref[rows, :]).astype(o_ref.dtype)
        return c

    lax.fori_loop(0, ATT_BLOCK // chunk, merge, 0)


def _alibi_table():
    hh = np.arange(1, ATT_HEADS + 1, dtype=np.float32)
    return jnp.exp2(-ALIBI_MAX_EXP * jnp.asarray(hh) / ATT_HEADS).reshape(N_GROUPS, HEADS_PER_GROUP)


def _rel_table():
    qi = np.arange(SUB)[:, None]
    kj = np.arange(2 * SUB)[None, :] - HALF
    return jnp.asarray(np.abs(qi - kj), F32)


def _attention(qkv0, qkv1, qkv2, gate, b, l):
    nblk = l // ATT_BLOCK
    qkv0 = qkv0.reshape(3, b, l, ATT_W)
    qkv1 = qkv1.reshape(3, b, 4, l // 4, ATT_W)
    qkv2 = qkv2.reshape(3, b, 16, l // 16, ATT_W)
    gate = gate.reshape(b, l, ATT_W)

    def specs(d):
        own = ATT_BLOCK // d
        hb = own // HALF
        nh = l // d // HALF
        lead = (None, None) if d == 1 else (None, None, d)
        zero = () if d == 1 else (0,)

        def mk(rows, which, fn):
            return pl.BlockSpec(lead + (rows, HEAD_DIM), lambda bi, i, h: (which, bi) + zero + (fn(i), h))

        cur = lambda i: i
        prv = lambda i: jnp.maximum(i * hb - 1, 0)
        nxt = lambda i: jnp.minimum((i + 1) * hb, nh - 1)
        return [mk(own, 0, cur),
                mk(HALF, 1, prv), mk(own, 1, cur), mk(HALF, 1, nxt),
                mk(HALF, 2, prv), mk(own, 2, cur), mk(HALF, 2, nxt)]

    in_specs = [pl.BlockSpec(memory_space=pltpu.SMEM),
                pl.BlockSpec((SUB, 2 * SUB), lambda bi, i, h: (0, 0))]
    in_specs += specs(1) + specs(4) + specs(16)
    in_specs += [pl.BlockSpec((None, ATT_BLOCK, HEAD_DIM), lambda bi, i, h: (bi, i, h))]
    return pl.pallas_call(
        _attn_kernel,
        grid=(b, nblk, HEADS_PER_GROUP),
        in_specs=in_specs,
        out_specs=pl.BlockSpec((None, ATT_BLOCK, HEAD_DIM), lambda bi, i, h: (bi, i, h)),
        out_shape=jax.ShapeDtypeStruct((b, l, ATT_W), BF16),
        scratch_shapes=[pltpu.VMEM((N_GROUPS, ATT_BLOCK, HEAD_DIM), F32)] * 3,
        compiler_params=_cparams("parallel", "parallel", "arbitrary"),
        name="dilated_attention",
    )(_alibi_table(), _rel_table(), *([qkv0] * 7), *([qkv1] * 7), *([qkv2] * 7), gate)


CONV_ROWS = 512
CONV_COLS = 1024
CONV_HALO = 8


def _conv_kernel(xp_ref, xc_ref, xn_ref, w_ref, b_ref, o_ref):
    i = pl.program_id(1)
    prev = jnp.where(i == 0, 0.0, xp_ref[...])
    nxt = jnp.where(i == pl.num_programs(1) - 1, 0.0, xn_ref[...])
    ext = jnp.concatenate([prev, xc_ref[...], nxt], axis=0)
    w = w_ref[...]
    acc = b_ref[...]
    n_ext = CONV_ROWS + 2 * CONV_HALO
    for k in range(D_CONV):
        shift = (D_CONV // 2 - k) % n_ext
        tap = ext if shift == 0 else pltpu.roll(ext, shift, axis=0)
        acc = acc + tap[CONV_HALO:CONV_HALO + CONV_ROWS] * w[k:k + 1]
    o_ref[...] = (acc * jax.nn.sigmoid(acc)).astype(o_ref.dtype)


def _conv(xbc, conv_w, conv_b, b, l, c0, c1, out_dtype):
    nb = l // CONV_ROWS
    hb = CONV_ROWS // CONV_HALO
    j0 = c0 // CONV_COLS
    nj = (c1 - c0) // CONV_COLS
    return pl.pallas_call(
        _conv_kernel,
        grid=(b, nb, nj),
        in_specs=[
            pl.BlockSpec((None, CONV_HALO, CONV_COLS), lambda bi, i, j: (bi, jnp.maximum(i * hb - 1, 0), j0 + j)),
            pl.BlockSpec((None, CONV_ROWS, CONV_COLS), lambda bi, i, j: (bi, i, j0 + j)),
            pl.BlockSpec((None, CONV_HALO, CONV_COLS),
                         lambda bi, i, j: (bi, jnp.minimum((i + 1) * hb, l // CONV_HALO - 1), j0 + j)),
            pl.BlockSpec((D_CONV, CONV_COLS), lambda bi, i, j: (0, j0 + j)),
            pl.BlockSpec((1, CONV_COLS), lambda bi, i, j: (0, j0 + j)),
        ],
        out_specs=pl.BlockSpec((None, CONV_ROWS, CONV_COLS), lambda bi, i, j: (bi, i, j)),
        out_shape=jax.ShapeDtypeStruct((b, l, c1 - c0), out_dtype),
        compiler_params=_cparams("parallel", "parallel", "parallel"),
        name="conv_silu",
    )(xbc, xbc, xbc, conv_w, conv_b)


SSD_Q = 128
SSD_BLOCK = 256


def _prefix_sum_rows(x):
    n = x.shape[0]
    row = lax.broadcasted_iota(jnp.int32, x.shape, 0)
    k = 1
    while k < n:
        x = x + jnp.where(row >= k, pltpu.roll(x, k, axis=0), 0.0)
        k *= 2
    return x


def _ssd_kernel(*refs, heads, groups, backward):
    if backward:
        x_ref, b_ref, c_ref, dt_ref, a_ref, e_ref, yf_ref, zs_ref, d_ref, w_ref, o_ref, r_ref, st_ref = refs
    else:
        x_ref, b_ref, c_ref, dt_ref, a_ref, e_ref, o_ref, st_ref = refs
    q = SSD_Q
    p = SSM_HEAD_DIM
    n = D_STATE
    hpg = heads // groups
    nc = x_ref.shape[0] // q

    @pl.when(pl.program_id(1) == 0)
    def _():
        st_ref[...] = jnp.zeros(st_ref.shape, F32)

    row = lax.broadcasted_iota(jnp.int32, (q, q), 0)
    col = lax.broadcasted_iota(jnp.int32, (q, q), 1)
    tri = row <= col if backward else row >= col
    a_all = -jnp.exp(a_ref[...])

    def split(v):
        hi = v.astype(BF16)
        lo = (v - hi.astype(F32)).astype(BF16)
        return jnp.concatenate([hi, lo], axis=1)

    log2e = math.log2(math.e)
    lane2 = lax.broadcasted_iota(jnp.int32, (q, 2 * p), 1)

    def chunk(c, carry):
        cc = nc - 1 - c if backward else c
        rows = pl.ds(pl.multiple_of(cc * q, q), q)
        dt = dt_ref[rows, :]
        da = dt * a_all
        pre = _prefix_sum_rows(da)
        tot = pre[q - 1:q, :]
        acs_all = tot - pre + da if backward else pre
        wx_all = jnp.dot(split(dt * jnp.exp(tot - acs_all)), e_ref[...],
                         preferred_element_type=F32)
        sc_all = jnp.dot(split(jnp.exp(acs_all)), e_ref[...],
                         preferred_element_type=F32)
        as2 = acs_all * log2e
        bs2 = (acs_all - jnp.log(dt)) * log2e
        bs_t = bs2.T
        ssq = jnp.zeros((q, 1), F32)
        cbs = [lax.dot_general(c_ref[rows, g * n:(g + 1) * n], b_ref[rows, g * n:(g + 1) * n],
                               (((1,), (1,)), ((), ())), preferred_element_type=F32) for g in range(groups)]
        for g in range(groups):
            cols = slice(g * hpg * p, (g + 1) * hpg * p)
            bg = b_ref[rows, g * n:(g + 1) * n]
            cg = c_ref[rows, g * n:(g + 1) * n]
            xg = x_ref[rows, cols]
            cb = cbs[g]
            outs = []
            for hh in range(0, hpg, 2):
                ls = []
                for h in (g * hpg + hh, g * hpg + hh + 1):
                    seg = as2[:, h:h + 1] - bs_t[h:h + 1, :]
                    ls.append((jnp.where(tri, jnp.exp2(seg), 0.0) * cb).astype(BF16))
                pair = xg[:, hh * p:(hh + 2) * p]
                zero = jnp.zeros_like(pair)
                rhs = jnp.concatenate([jnp.where(lane2 < p, pair, zero), jnp.where(lane2 >= p, pair, zero)], axis=0)
                outs.append(jnp.dot(jnp.concatenate(ls, axis=1), rhs, preferred_element_type=F32))
            sc = sc_all[:, cols]
            wx = wx_all[:, cols]
            cdx = sc[0:1, :] if backward else sc[q - 1:q, :]
            st = st_ref[g]
            y_off = jnp.dot(cg, st.astype(BF16), preferred_element_type=F32) * sc
            yg = jnp.concatenate(outs, axis=1) + y_off
            xf = xg.astype(F32)
            if backward:
                yg = (yg + yf_ref[rows, cols] + d_ref[:, cols] * xf) * zs_ref[rows, cols]
                ssq = ssq + jnp.sum(yg * yg, axis=-1, keepdims=True)
                o_ref[rows, cols] = (yg * w_ref[:, cols]).astype(o_ref.dtype)
            else:
                o_ref[rows, cols] = yg
            xdec = (xf * wx).astype(BF16)
            new = lax.dot_general(bg, xdec, (((0,), (0,)), ((), ())), preferred_element_type=F32)
            st_ref[g] = st * cdx + new
        if backward:
            inv = lax.rsqrt(ssq * (1.0 / (heads * p)) + NORM_EPS)
            r_ref[rows, :] = jnp.broadcast_to(inv, (q, r_ref.shape[-1]))
        return carry

    lax.fori_loop(0, nc, chunk, 0)


def _ssd_pass(xbc, dt, a_log_row, post, b, l, hp, backward):
    gn = SSM_GROUPS * D_STATE
    assert hp % gn == 0
    heads = hp // SSM_HEAD_DIM
    nblk = l // SSD_BLOCK
    e = np.zeros((4 * heads, hp), np.float32)
    for h in range(heads):
        e[h, h * SSM_HEAD_DIM:(h + 1) * SSM_HEAD_DIM] = 1.0
        e[2 * heads + h, h * SSM_HEAD_DIM:(h + 1) * SSM_HEAD_DIM] = 1.0
    const = lambda bi, i: (0, 0)

    def tok(w, cblk=0):
        return pl.BlockSpec((None, SSD_BLOCK, w), lambda bi, i: (bi, nblk - 1 - i if backward else i, cblk))

    in_specs = [tok(hp), tok(gn, hp // gn), tok(gn, hp // gn + 1), tok(2 * heads, 1 if backward else 0),
                pl.BlockSpec((1, 2 * heads), lambda bi, i: (0, 1 if backward else 0)),
                pl.BlockSpec((4 * heads, hp), const)]
    args = [xbc, xbc, xbc, dt, a_log_row, jnp.asarray(e, BF16)]
    scratch = [pltpu.VMEM((SSM_GROUPS, D_STATE, hp // SSM_GROUPS), F32)]
    if backward:
        in_specs += [tok(hp), tok(hp), pl.BlockSpec((1, hp), const), pl.BlockSpec((1, hp), const)]
        args += list(post)
        out_specs = [tok(hp), tok(LANES)]
        out_shape = [jax.ShapeDtypeStruct((b, l, hp), BF16), jax.ShapeDtypeStruct((b, l, LANES), F32)]
    else:
        out_specs = tok(hp)
        out_shape = jax.ShapeDtypeStruct((b, l, hp), F32)
    return pl.pallas_call(
        functools.partial(_ssd_kernel, heads=heads, groups=SSM_GROUPS, backward=backward),
        grid=(b, nblk),
        in_specs=in_specs,
        out_specs=out_specs,
        out_shape=out_shape,
        scratch_shapes=scratch,
        compiler_params=_cparams("parallel", "arbitrary"),
        name="ssd_scan_bwd" if backward else "ssd_scan_fwd",
    )(*args)


def _merge_kernel(att_ref, ssm_ref, r_ref, wa_ref, ws_ref, ga_ref, gs_ref, o_ref):
    a = jnp.dot(att_ref[...], wa_ref[...], preferred_element_type=F32)
    s = jnp.dot(ssm_ref[...], ws_ref[...], preferred_element_type=F32) * r_ref[:, 0:1]
    o_ref[...] = (ga_ref[...] * a + gs_ref[...] * s).astype(o_ref.dtype)


def _merge(att, ssm_in, ssm_r, wa, ws, gates, tm=512, tn=1024):
    m = att.shape[0]
    n = wa.shape[1]
    nj = n // tn
    return pl.pallas_call(
        _merge_kernel,
        grid=(nj, m // tm),
        in_specs=[
            pl.BlockSpec((tm, att.shape[1]), lambda j, i: (i, 0)),
            pl.BlockSpec((tm, ssm_in.shape[1]), lambda j, i: (i, 0)),
            pl.BlockSpec((tm, LANES), lambda j, i: (i, 0)),
            pl.BlockSpec((wa.shape[0], tn), lambda j, i: (0, j)),
            pl.BlockSpec((ws.shape[0], tn), lambda j, i: (0, j)),
            pl.BlockSpec((tm, tn), lambda j, i: (i, j)),
            pl.BlockSpec((tm, tn), lambda j, i: (i, nj + j)),
        ],
        out_specs=pl.BlockSpec((tm, tn), lambda j, i: (i, j)),
        out_shape=jax.ShapeDtypeStruct((m, n), BF16),
        compiler_params=_cparams("parallel", "parallel"),
        name="branch_merge",
    )(att, ssm_in, ssm_r, wa, ws, gates, gates)


def _out_kernel(m_ref, w_ref, x_ref, nw_ref, o_ref):
    y = x_ref[...] + jnp.dot(m_ref[...], w_ref[...], preferred_element_type=F32)
    y = y * lax.rsqrt(jnp.mean(y * y, axis=-1, keepdims=True) + NORM_EPS)
    o_ref[...] = y * nw_ref[...]


def _out_proj(merged, w, x, nw_row, tm=512):
    m, n = x.shape
    return pl.pallas_call(
        _out_kernel,
        grid=(m // tm,),
        in_specs=[
            pl.BlockSpec((tm, merged.shape[1]), lambda i: (i, 0)),
            pl.BlockSpec(w.shape, lambda i: (0, 0)),
            pl.BlockSpec((tm, n), lambda i: (i, 0)),
            pl.BlockSpec((1, n), lambda i: (0, 0)),
        ],
        out_specs=pl.BlockSpec((tm, n), lambda i: (i, 0)),
        out_shape=jax.ShapeDtypeStruct((m, n), F32),
        compiler_params=_cparams("parallel"),
        name="out_proj_norm",
    )(merged, w, x, nw_row)


def kernel(x_prompt, x_sample, norm_w, w_in, b_gate, conv_w, conv_b, dt_bias, a_log, d_skip, ssm_norm_w, w_attn_o, w_ssm_o, w_out, final_norm_w):
    d_model = x_prompt.shape[-1]
    d_inner = ssm_norm_w.shape[-1]
    conv_ch = conv_w.shape[-1]
    ssm_heads = d_skip.shape[-1]
    offs = np.cumsum([0, QKV_W, QKV_W, QKV_W, ATT_W, d_inner, conv_ch, 2 * ssm_heads, N_BRANCH * d_model])
    w = w_in[0].astype(BF16)
    both = lambda t: jnp.concatenate([t, t[..., ssm_heads:], t[..., :ssm_heads]], axis=-1)
    w_dt = both(w[:, offs[6]:offs[7]])
    dt_b = both(dt_bias[0].astype(F32).reshape(1, -1))
    w_gates = w[:, offs[7]:offs[8]]
    zero_b = jnp.zeros((1, conv_ch), F32)
    wa, ws, wo = w_attn_o[0].astype(BF16), w_ssm_o[0].astype(BF16), w_out[0].astype(BF16)
    a_log_row = both(a_log[0].astype(F32).reshape(1, -1))
    d_row = jnp.repeat(d_skip[0].astype(F32), SSM_HEAD_DIM).reshape(1, -1)
    conv_b = conv_b[0].reshape(1, -1)
    col_tile = 1024
    assert offs[4] % ATT_W == 0 and offs[4] % col_tile == 0 and offs[5] % col_tile == 0

    outs = []
    for x in (x_prompt, x_sample):
        b, l, _ = x.shape
        hn, hn4, hn16 = _prologue(x, norm_w[0])
        hn = hn.reshape(b * l, d_model)
        qkv = [_proj(h_g.reshape(b * l, d_model), w, zero_b, mode="qkv", tn=ATT_W, n=3 * ATT_W, col0=g,
                     col_step=N_GROUPS, out_dtype=BF16) for g, h_g in enumerate((hn, hn4, hn16))]
        gatt = _proj(hn, w, zero_b, mode="silu", tn=ATT_W, n=ATT_W, col0=offs[3] // ATT_W, out_dtype=F32)
        zs = _proj(hn, w, zero_b, mode="silu", tn=col_tile, n=d_inner, col0=offs[4] // col_tile, out_dtype=F32)
        xbc = _proj(hn, w, zero_b, mode="raw", tn=col_tile, n=conv_ch, col0=offs[5] // col_tile, out_dtype=F32)
        dt = _proj(hn, w_dt, dt_b, mode="softplus", tn=4 * ssm_heads, n=4 * ssm_heads, out_dtype=F32)
        gates = _proj(hn, w_gates, b_gate[0].reshape(1, -1), mode="sigmoid", tn=col_tile, n=N_BRANCH * d_model,
                      out_dtype=F32)
        att = _attention(qkv[0], qkv[1], qkv[2], gatt, b, l).reshape(b * l, ATT_W)
        xbc = _conv(xbc.reshape(b, l, conv_ch), conv_w[0], conv_b, b, l, 0, conv_ch, BF16)
        dt = dt.reshape(b, l, 4 * ssm_heads)
        yf = _ssd_pass(xbc, dt, a_log_row, None, b, l, d_inner, backward=False)
        post = (yf, zs.reshape(b, l, d_inner), d_row, ssm_norm_w[0].reshape(1, -1))
        ssm_in, ssm_r = _ssd_pass(xbc, dt, a_log_row, post, b, l, d_inner, backward=True)
        merged = _merge(att, ssm_in.reshape(b * l, d_inner), ssm_r.reshape(b * l, LANES), wa, ws, gates)
        out = _out_proj(merged, wo, x.reshape(b * l, d_model), final_norm_w.reshape(1, -1))
        outs.append(out.reshape(b, l, d_model))
    return tuple(outs)
```

```python
import functools
import math

import numpy as np
import jax
import jax.numpy as jnp
from jax import lax
from jax.experimental import pallas as pl
from jax.experimental.pallas import tpu as pltpu

F32 = jnp.float32
BF16 = jnp.bfloat16

NORM_EPS = 1e-5
ATT_GROUPS = ((128, 1), (512, 4), (2048, 16))
N_GROUPS = 3
HEADS_PER_GROUP = 12
HEAD_DIM = 128
ATT_HEADS = N_GROUPS * HEADS_PER_GROUP
ATT_W = HEADS_PER_GROUP * HEAD_DIM
QKV_W = ATT_HEADS * HEAD_DIM
ALIBI_MAX_EXP = 8.0
HALF = 64
SSM_HEAD_DIM = 64
SSM_GROUPS = 8
D_STATE = 128
D_CONV = 5
N_BRANCH = 2

VMEM_LIMIT_BYTES = 56 * 1024 * 1024
LANES = 128
PERM_BLOCK = 256
ATT_BLOCK = 2048
SUB = 128
NEG_BIG = -1e30


def _cparams(*sem):
    return pltpu.CompilerParams(dimension_semantics=sem, vmem_limit_bytes=VMEM_LIMIT_BYTES)


def _perm_matrix(d):
    n = PERM_BLOCK // d
    p = np.zeros((PERM_BLOCK, PERM_BLOCK), np.float32)
    for r in range(d):
        for i in range(n):
            p[r * n + i, i * d + r] = 1.0
    return jnp.asarray(p, BF16)


def _prologue_kernel(x_ref, w_ref, p4_ref, p16_ref, nat_ref, o4_ref, o16_ref):
    x = x_ref[...]
    y = x * lax.rsqrt(jnp.mean(x * x, axis=-1, keepdims=True) + NORM_EPS)
    yb = (y * w_ref[...]).astype(BF16)
    nat_ref[...] = yb
    y4 = jnp.dot(p4_ref[...], yb, preferred_element_type=F32).astype(BF16)
    n4 = PERM_BLOCK // 4
    for r in range(4):
        o4_ref[r] = y4[r * n4:(r + 1) * n4]
    y16 = jnp.dot(p16_ref[...], yb, preferred_element_type=F32).astype(BF16)
    n16 = PERM_BLOCK // 16
    for r in range(16):
        o16_ref[r] = y16[r * n16:(r + 1) * n16]


def _prologue(x, norm_w):
    b, l, d = x.shape
    nblk = l // PERM_BLOCK
    return pl.pallas_call(
        _prologue_kernel,
        grid=(b, nblk),
        in_specs=[
            pl.BlockSpec((None, PERM_BLOCK, d), lambda bi, i: (bi, i, 0)),
            pl.BlockSpec((1, d), lambda bi, i: (0, 0)),
            pl.BlockSpec((PERM_BLOCK, PERM_BLOCK), lambda bi, i: (0, 0)),
            pl.BlockSpec((PERM_BLOCK, PERM_BLOCK), lambda bi, i: (0, 0)),
        ],
        out_specs=[
            pl.BlockSpec((None, PERM_BLOCK, d), lambda bi, i: (bi, i, 0)),
            pl.BlockSpec((None, 4, PERM_BLOCK // 4, d), lambda bi, i: (bi, 0, i, 0)),
            pl.BlockSpec((None, 16, PERM_BLOCK // 16, d), lambda bi, i: (bi, 0, i, 0)),
        ],
        out_shape=[
            jax.ShapeDtypeStruct((b, l, d), BF16),
            jax.ShapeDtypeStruct((b, 4, l // 4, d), BF16),
            jax.ShapeDtypeStruct((b, 16, l // 16, d), BF16),
        ],
        compiler_params=_cparams("parallel", "parallel"),
        name="prologue_norm_permute",
    )(x, norm_w.reshape(1, d), _perm_matrix(4), _perm_matrix(16))


def _proj_kernel(x_ref, w_ref, b_ref, o_ref, *, mode):
    acc = jnp.dot(x_ref[...], w_ref[...], preferred_element_type=F32)
    if mode == "qkv":
        scale = jnp.where(pl.program_id(0) == 0, HEAD_DIM ** -0.5, 1.0).astype(F32)
        acc = acc * scale
    elif mode == "silu":
        acc = acc * jax.nn.sigmoid(acc)
    elif mode == "softplus":
        acc = jax.nn.softplus(acc + b_ref[...])
    elif mode == "sigmoid":
        acc = jax.nn.sigmoid(acc + b_ref[...])
    o_ref[...] = acc.astype(o_ref.dtype)


def _proj(x, w, bias, *, mode, tn, n, out_dtype, col0=0, col_step=1, tm=1024):
    m, k = x.shape
    tm = min(tm, m)
    nj, ni = n // tn, m // tm
    if mode == "qkv":
        out_shape = jax.ShapeDtypeStruct((nj, m, tn), out_dtype)
        out_spec = pl.BlockSpec((None, tm, tn), lambda j, i: (j, i, 0))
    else:
        out_shape = jax.ShapeDtypeStruct((m, n), out_dtype)
        out_spec = pl.BlockSpec((tm, tn), lambda j, i: (i, j))
    return pl.pallas_call(
        functools.partial(_proj_kernel, mode=mode),
        grid=(nj, ni),
        in_specs=[
            pl.BlockSpec((tm, k), lambda j, i: (i, 0)),
            pl.BlockSpec((k, tn), lambda j, i: (0, col0 + col_step * j)),
            pl.BlockSpec((1, tn), lambda j, i: (0, j)),
        ],
        out_specs=out_spec,
        out_shape=out_shape,
        compiler_params=_cparams("parallel", "parallel"),
        name="in_proj_" + mode,
    )(x, w, bias)


def _attn_kernel(sl_ref, rel_ref,
                 q0, k0p, k0c, k0n, v0p, v0c, v0n,
                 q1, k1p, k1c, k1n, v1p, v1c, v1n,
                 q2, k2p, k2c, k2n, v2p, v2c, v2n,
                 gate_ref, o_ref, acc_s, mb_s, lb_s):
    i = pl.program_id(1)
    h = pl.program_id(2)
    is_first = i == 0
    is_last = i == pl.num_programs(1) - 1
    ones = jnp.ones((2 * SUB, HEAD_DIM), BF16)

    def window(p_ref, c_ref, n_ref, lead, j, nsub):
        lo, hi = j * SUB - HALF, (j + 1) * SUB + HALF
        parts = []
        if j == 0:
            parts.append(p_ref[lead + (slice(None), slice(None))])
            lo = 0
        own_hi = min(hi, nsub * SUB)
        parts.append(c_ref[lead + (pl.ds(lo, own_hi - lo), slice(None))])
        if j == nsub - 1:
            parts.append(n_ref[lead + (slice(None), slice(None))])
        return parts[0] if len(parts) == 1 else jnp.concatenate(parts, axis=0)

    absrel = rel_ref[...]
    col = lax.broadcasted_iota(jnp.int32, (SUB, 2 * SUB), 1)
    in_window = absrel <= float(HALF)
    mask_lo = jnp.logical_and(is_first, col < HALF)
    mask_hi = jnp.logical_and(is_last, col >= HALF + SUB)

    def bias(g):
        dist = absrel * float(ATT_GROUPS[g][1])
        return jnp.where(in_window, -(sl_ref[g, h] * dist), NEG_BIG)

    def attend(g, q, k, v, nb, rows):
        s = lax.dot_general(q, k, (((1,), (1,)), ((), ())), preferred_element_type=F32) + nb
        m = jnp.max(s, axis=-1, keepdims=True)
        p = jnp.exp(s - m).astype(BF16)
        al = jnp.dot(p, jnp.concatenate([v, ones], axis=1), preferred_element_type=F32)
        acc_s[g, rows, :] = al[:, :HEAD_DIM]
        lb_s[g, rows, :] = al[:, HEAD_DIM:]
        mb_s[g, rows, :] = jnp.broadcast_to(m, (SUB, HEAD_DIM))

    def group(g, q_ref, kp, kc, kn, vp, vc, vn, lead, r, nsub):
        d = ATT_GROUPS[g][1]
        nb = nbs[g]
        for j in range(nsub):
            nbj = nb
            if j == 0:
                nbj = jnp.where(mask_lo, NEG_BIG, nbj)
            if j == nsub - 1:
                nbj = jnp.where(mask_hi, NEG_BIG, nbj)
            q = q_ref[lead + (pl.ds(j * SUB, SUB), slice(None))]
            rows = pl.ds(j * SUB, SUB) if d == 1 else pl.ds(j * SUB * d + r, SUB, stride=d)
            attend(g, q, window(kp, kc, kn, lead, j, nsub), window(vp, vc, vn, lead, j, nsub), nbj, rows)

    nbs = [bias(g) for g in range(N_GROUPS)]
    group(0, q0, k0p, k0c, k0n, v0p, v0c, v0n, (), 0, ATT_BLOCK // SUB)

    def g1_body(it, c):
        for rr in range(2):
            r = it * 2 + rr
            group(1, q1, k1p, k1c, k1n, v1p, v1c, v1n, (r,), r, ATT_BLOCK // 4 // SUB)
        return c

    lax.fori_loop(0, 2, g1_body, 0)

    def g2_body(it, c):
        for rr in range(8):
            r = it * 8 + rr
            group(2, q2, k2p, k2c, k2n, v2p, v2c, v2n, (r,), r, 1)
        return c

    lax.fori_loop(0, 2, g2_body, 0)

    chunk = 256

    def merge(cix, c):
        rows = pl.ds(pl.multiple_of(cix * chunk, chunk), chunk)
        m0, m1, m2 = mb_s[0, rows, :], mb_s[1, rows, :], mb_s[2, rows, :]
        mm = jnp.maximum(jnp.maximum(m0, m1), m2)
        w0, w1, w2 = jnp.exp(m0 - mm), jnp.exp(m1 - mm), jnp.exp(m2 - mm)
        num = w0 * acc_s[0, rows, :] + w1 * acc_s[1, rows, :] + w2 * acc_s[2, rows, :]
        den = w0 * lb_s[0, rows, :] + w1 * lb_s[1, rows, :] + w2 * lb_s[2, rows, :]
        o_ref[rows, :] = (num / den * gate_ref[rows, :]).astype(o_ref.dtype)
        return c

    lax.fori_loop(0, ATT_BLOCK // chunk, merge, 0)


def _alibi_table():
    hh = np.arange(1, ATT_HEADS + 1, dtype=np.float32)
    return jnp.exp2(-ALIBI_MAX_EXP * jnp.asarray(hh) / ATT_HEADS).reshape(N_GROUPS, HEADS_PER_GROUP)


def _rel_table():
    qi = np.arange(SUB)[:, None]
    kj = np.arange(2 * SUB)[None, :] - HALF
    return jnp.asarray(np.abs(qi - kj), F32)


def _attention(qkv0, qkv1, qkv2, gate, b, l):
    nblk = l // ATT_BLOCK
    qkv0 = qkv0.reshape(3, b, l, ATT_W)
    qkv1 = qkv1.reshape(3, b, 4, l // 4, ATT_W)
    qkv2 = qkv2.reshape(3, b, 16, l // 16, ATT_W)
    gate = gate.reshape(b, l, ATT_W)

    def specs(d):
        own = ATT_BLOCK // d
        hb = own // HALF
        nh = l // d // HALF
        lead = (None, None) if d == 1 else (None, None, d)
        zero = () if d == 1 else (0,)

        def mk(rows, which, fn):
            return pl.BlockSpec(lead + (rows, HEAD_DIM), lambda bi, i, h: (which, bi) + zero + (fn(i), h))

        cur = lambda i: i
        prv = lambda i: jnp.maximum(i * hb - 1, 0)
        nxt = lambda i: jnp.minimum((i + 1) * hb, nh - 1)
        return [mk(own, 0, cur),
                mk(HALF, 1, prv), mk(own, 1, cur), mk(HALF, 1, nxt),
                mk(HALF, 2, prv), mk(own, 2, cur), mk(HALF, 2, nxt)]

    in_specs = [pl.BlockSpec(memory_space=pltpu.SMEM),
                pl.BlockSpec((SUB, 2 * SUB), lambda bi, i, h: (0, 0))]
    in_specs += specs(1) + specs(4) + specs(16)
    in_specs += [pl.BlockSpec((None, ATT_BLOCK, HEAD_DIM), lambda bi, i, h: (bi, i, h))]
    return pl.pallas_call(
        _attn_kernel,
        grid=(b, nblk, HEADS_PER_GROUP),
        in_specs=in_specs,
        out_specs=pl.BlockSpec((None, ATT_BLOCK, HEAD_DIM), lambda bi, i, h: (bi, i, h)),
        out_shape=jax.ShapeDtypeStruct((b, l, ATT_W), BF16),
        scratch_shapes=[pltpu.VMEM((N_GROUPS, ATT_BLOCK, HEAD_DIM), F32)] * 3,
        compiler_params=_cparams("parallel", "parallel", "arbitrary"),
        name="dilated_attention",
    )(_alibi_table(), _rel_table(), *([qkv0] * 7), *([qkv1] * 7), *([qkv2] * 7), gate)


CONV_ROWS = 512
CONV_COLS = 1024
CONV_HALO = 8


def _conv_kernel(xp_ref, xc_ref, xn_ref, w_ref, b_ref, o_ref):
    i = pl.program_id(1)
    prev = jnp.where(i == 0, 0.0, xp_ref[...])
    nxt = jnp.where(i == pl.num_programs(1) - 1, 0.0, xn_ref[...])
    ext = jnp.concatenate([prev, xc_ref[...], nxt], axis=0)
    w = w_ref[...]
    acc = b_ref[...]
    n_ext = CONV_ROWS + 2 * CONV_HALO
    for k in range(D_CONV):
        shift = (D_CONV // 2 - k) % n_ext
        tap = ext if shift == 0 else pltpu.roll(ext, shift, axis=0)
        acc = acc + tap[CONV_HALO:CONV_HALO + CONV_ROWS] * w[k:k + 1]
    o_ref[...] = (acc * jax.nn.sigmoid(acc)).astype(o_ref.dtype)


def _conv(xbc, conv_w, conv_b, b, l, c0, c1, out_dtype):
    nb = l // CONV_ROWS
    hb = CONV_ROWS // CONV_HALO
    j0 = c0 // CONV_COLS
    nj = (c1 - c0) // CONV_COLS
    return pl.pallas_call(
        _conv_kernel,
        grid=(b, nb, nj),
        in_specs=[
            pl.BlockSpec((None, CONV_HALO, CONV_COLS), lambda bi, i, j: (bi, jnp.maximum(i * hb - 1, 0), j0 + j)),
            pl.BlockSpec((None, CONV_ROWS, CONV_COLS), lambda bi, i, j: (bi, i, j0 + j)),
            pl.BlockSpec((None, CONV_HALO, CONV_COLS),
                         lambda bi, i, j: (bi, jnp.minimum((i + 1) * hb, l // CONV_HALO - 1), j0 + j)),
            pl.BlockSpec((D_CONV, CONV_COLS), lambda bi, i, j: (0, j0 + j)),
            pl.BlockSpec((1, CONV_COLS), lambda bi, i, j: (0, j0 + j)),
        ],
        out_specs=pl.BlockSpec((None, CONV_ROWS, CONV_COLS), lambda bi, i, j: (bi, i, j)),
        out_shape=jax.ShapeDtypeStruct((b, l, c1 - c0), out_dtype),
        compiler_params=_cparams("parallel", "parallel", "parallel"),
        name="conv_silu",
    )(xbc, xbc, xbc, conv_w, conv_b)


SSD_Q = 128
SSD_BLOCK = 256


def _bf16_terms(v, n):
    out = []
    for _ in range(n):
        t = v.astype(BF16).astype(F32)
        out.append(t)
        v = v - t
    return out


def _prefix_sum_rows(x):
    n = x.shape[0]
    row = lax.broadcasted_iota(jnp.int32, x.shape, 0)
    k = 1
    while k < n:
        x = x + jnp.where(row >= k, pltpu.roll(x, k, axis=0), 0.0)
        k *= 2
    return x


def _ssd_kernel(*refs, heads, groups, backward):
    if backward:
        x_ref, b_ref, c_ref, dt_ref, a_ref, e_ref, yf_ref, zs_ref, d_ref, w_ref, o_ref, r_ref, st_ref = refs
    else:
        x_ref, b_ref, c_ref, dt_ref, a_ref, e_ref, o_ref, st_ref = refs
    q = SSD_Q
    p = SSM_HEAD_DIM
    n = D_STATE
    hpg = heads // groups
    nc = x_ref.shape[0] // q

    @pl.when(pl.program_id(1) == 0)
    def _():
        st_ref[...] = jnp.zeros(st_ref.shape, F32)

    row = lax.broadcasted_iota(jnp.int32, (q, q), 0)
    col = lax.broadcasted_iota(jnp.int32, (q, q), 1)
    lower, upper = row >= col, row <= col
    is_fwd_lane = lax.broadcasted_iota(jnp.int32, (q, 2 * heads), 1) < heads
    a_all = -jnp.exp(a_ref[...])

    def split(v):
        return jnp.concatenate(_bf16_terms(v, 2), axis=1).astype(BF16)

    log2e = math.log2(math.e)
    lane2 = lax.broadcasted_iota(jnp.int32, (q, 2 * p), 1)

    def chunk(c, carry):
        cc = nc - 1 - c if backward else c
        rows = pl.ds(pl.multiple_of(cc * q, q), q)
        dt = dt_ref[rows, :]
        da = dt * a_all
        pre = _prefix_sum_rows(da)
        tot = pre[q - 1:q, :]
        acs_all = jnp.where(is_fwd_lane, pre, tot - pre + da)
        wx_all = jnp.dot(split(dt * jnp.exp(tot - acs_all)), e_ref[...],
                         preferred_element_type=F32)
        sc_all = jnp.dot(split(jnp.exp(acs_all)), e_ref[...],
                         preferred_element_type=F32)
        ssq = jnp.zeros((q, 1), F32)
        if not backward:
            as2 = acs_all * log2e
            bs2 = (acs_all - jnp.log(dt)) * log2e
            bs_t = bs2.T
            cbs = [lax.dot_general(c_ref[rows, g * n:(g + 1) * n], b_ref[rows, g * n:(g + 1) * n],
                                   (((1,), (1,)), ((), ())), preferred_element_type=F32) for g in range(groups)]
        for g in range(groups):
            cols = slice(g * hpg * p, (g + 1) * hpg * p)
            bg = b_ref[rows, g * n:(g + 1) * n]
            cg = c_ref[rows, g * n:(g + 1) * n]
            xg = x_ref[rows, cols]
            sc = sc_all[:, cols]
            wx = wx_all[:, cols]
            cdx = sc[0:1, :] if backward else sc[q - 1:q, :]
            st = st_ref[g]
            yg = jnp.dot(cg, st.astype(BF16), preferred_element_type=F32) * sc
            if not backward:
                for hh in range(0, hpg, 2):
                    ls = []
                    for h in (g * hpg + hh, g * hpg + hh + 1):
                        hb = heads + h
                        ef = jnp.exp2(as2[:, h:h + 1] - bs_t[h:h + 1, :])
                        eb = jnp.exp2(as2[:, hb:hb + 1] - bs_t[hb:hb + 1, :])
                        dec = jnp.where(lower, ef, 0.0) + jnp.where(upper, eb, 0.0)
                        ls.append((dec * cbs[g]).astype(BF16))
                    pair = xg[:, hh * p:(hh + 2) * p]
                    zero = jnp.zeros_like(pair)
                    rhs = jnp.concatenate([jnp.where(lane2 < p, pair, zero), jnp.where(lane2 >= p, pair, zero)],
                                          axis=0)
                    yd = jnp.dot(jnp.concatenate(ls, axis=1), rhs, preferred_element_type=F32)
                    pc = slice(hh * p, (hh + 2) * p)
                    o_ref[rows, g * hpg * p + hh * p:g * hpg * p + (hh + 2) * p] = yg[:, pc] + yd
            xf = xg.astype(F32)
            if backward:
                yg = (yg + yf_ref[rows, cols] + d_ref[:, cols] * xf) * zs_ref[rows, cols]
                ssq = ssq + jnp.sum(yg * yg, axis=-1, keepdims=True)
                o_ref[rows, cols] = (yg * w_ref[:, cols]).astype(o_ref.dtype)
            xdec = (xf * wx).astype(BF16)
            new = lax.dot_general(bg, xdec, (((0,), (0,)), ((), ())), preferred_element_type=F32)
            st_ref[g] = st * cdx + new
        if backward:
            inv = lax.rsqrt(ssq * (1.0 / (heads * p)) + NORM_EPS)
            r_ref[rows, :] = jnp.broadcast_to(inv, (q, r_ref.shape[-1]))
        return carry

    lax.fori_loop(0, nc, chunk, 0)


def _ssd_pass(xbc, dt, a_log_row, post, b, l, hp, backward):
    gn = SSM_GROUPS * D_STATE
    assert hp % gn == 0
    heads = hp // SSM_HEAD_DIM
    nblk = l // SSD_BLOCK
    e = np.zeros((4 * heads, hp), np.float32)
    off = heads if backward else 0
    for h in range(heads):
        e[off + h, h * SSM_HEAD_DIM:(h + 1) * SSM_HEAD_DIM] = 1.0
        e[2 * heads + off + h, h * SSM_HEAD_DIM:(h + 1) * SSM_HEAD_DIM] = 1.0
    const = lambda bi, i: (0, 0)

    def tok(w, cblk=0):
        return pl.BlockSpec((None, SSD_BLOCK, w), lambda bi, i: (bi, nblk - 1 - i if backward else i, cblk))

    in_specs = [tok(hp), tok(gn, hp // gn), tok(gn, hp // gn + 1), tok(2 * heads),
                pl.BlockSpec((1, 2 * heads), const), pl.BlockSpec((4 * heads, hp), const)]
    args = [xbc, xbc, xbc, dt, a_log_row, jnp.asarray(e, BF16)]
    scratch = [pltpu.VMEM((SSM_GROUPS, D_STATE, hp // SSM_GROUPS), F32)]
    if backward:
        in_specs += [tok(hp), tok(hp), pl.BlockSpec((1, hp), const), pl.BlockSpec((1, hp), const)]
        args += list(post)
        out_specs = [tok(hp), tok(LANES)]
        out_shape = [jax.ShapeDtypeStruct((b, l, hp), BF16), jax.ShapeDtypeStruct((b, l, LANES), F32)]
    else:
        out_specs = tok(hp)
        out_shape = jax.ShapeDtypeStruct((b, l, hp), F32)
    return pl.pallas_call(
        functools.partial(_ssd_kernel, heads=heads, groups=SSM_GROUPS, backward=backward),
        grid=(b, nblk),
        in_specs=in_specs,
        out_specs=out_specs,
        out_shape=out_shape,
        scratch_shapes=scratch,
        compiler_params=_cparams("parallel", "arbitrary"),
        name="ssd_scan_bwd" if backward else "ssd_scan_fwd",
    )(*args)


def _merge_kernel(att_ref, ssm_ref, r_ref, wa_ref, ws_ref, ga_ref, gs_ref, o_ref):
    a = jnp.dot(att_ref[...], wa_ref[...], preferred_element_type=F32)
    s = jnp.dot(ssm_ref[...], ws_ref[...], preferred_element_type=F32) * r_ref[:, 0:1]
    o_ref[...] = (ga_ref[...] * a + gs_ref[...] * s).astype(o_ref.dtype)


def _merge(att, ssm_in, ssm_r, wa, ws, gates, tm=512, tn=1024):
    m = att.shape[0]
    n = wa.shape[1]
    nj = n // tn
    return pl.pallas_call(
        _merge_kernel,
        grid=(nj, m // tm),
        in_specs=[
            pl.BlockSpec((tm, att.shape[1]), lambda j, i: (i, 0)),
            pl.BlockSpec((tm, ssm_in.shape[1]), lambda j, i: (i, 0)),
            pl.BlockSpec((tm, LANES), lambda j, i: (i, 0)),
            pl.BlockSpec((wa.shape[0], tn), lambda j, i: (0, j)),
            pl.BlockSpec((ws.shape[0], tn), lambda j, i: (0, j)),
            pl.BlockSpec((tm, tn), lambda j, i: (i, j)),
            pl.BlockSpec((tm, tn), lambda j, i: (i, nj + j)),
        ],
        out_specs=pl.BlockSpec((tm, tn), lambda j, i: (i, j)),
        out_shape=jax.ShapeDtypeStruct((m, n), BF16),
        compiler_params=_cparams("parallel", "parallel"),
        name="branch_merge",
    )(att, ssm_in, ssm_r, wa, ws, gates, gates)


def _out_kernel(m_ref, w_ref, x_ref, nw_ref, o_ref):
    y = x_ref[...] + jnp.dot(m_ref[...], w_ref[...], preferred_element_type=F32)
    y = y * lax.rsqrt(jnp.mean(y * y, axis=-1, keepdims=True) + NORM_EPS)
    o_ref[...] = y * nw_ref[...]


def _out_proj(merged, w, x, nw_row, tm=512):
    m, n = x.shape
    return pl.pallas_call(
        _out_kernel,
        grid=(m // tm,),
        in_specs=[
            pl.BlockSpec((tm, merged.shape[1]), lambda i: (i, 0)),
            pl.BlockSpec(w.shape, lambda i: (0, 0)),
            pl.BlockSpec((tm, n), lambda i: (i, 0)),
            pl.BlockSpec((1, n), lambda i: (0, 0)),
        ],
        out_specs=pl.BlockSpec((tm, n), lambda i: (i, 0)),
        out_shape=jax.ShapeDtypeStruct((m, n), F32),
        compiler_params=_cparams("parallel"),
        name="out_proj_norm",
    )(merged, w, x, nw_row)


def kernel(x_prompt, x_sample, norm_w, w_in, b_gate, conv_w, conv_b, dt_bias, a_log, d_skip, ssm_norm_w, w_attn_o, w_ssm_o, w_out, final_norm_w):
    d_model = x_prompt.shape[-1]
    d_inner = ssm_norm_w.shape[-1]
    conv_ch = conv_w.shape[-1]
    ssm_heads = d_skip.shape[-1]
    offs = np.cumsum([0, QKV_W, QKV_W, QKV_W, ATT_W, d_inner, conv_ch, 2 * ssm_heads, N_BRANCH * d_model])
    w = w_in[0].astype(BF16)
    w_dt = w[:, offs[6]:offs[7]]
    dt_b = dt_bias[0].astype(F32).reshape(1, -1)
    w_z, w_xbc, w_gates = w[:, offs[4]:offs[5]], w[:, offs[5]:offs[6]], w[:, offs[7]:offs[8]]
    wide_tile = 2048
    zero_b = jnp.zeros((1, conv_ch), F32)
    wa, ws, wo = w_attn_o[0].astype(BF16), w_ssm_o[0].astype(BF16), w_out[0].astype(BF16)
    a_log_row = a_log[0].astype(F32).reshape(1, -1)
    d_row = jnp.repeat(d_skip[0].astype(F32), SSM_HEAD_DIM).reshape(1, -1)
    conv_b = conv_b[0].reshape(1, -1)
    assert offs[3] % ATT_W == 0

    outs = []
    for x in (x_prompt, x_sample):
        b, l, _ = x.shape
        hn, hn4, hn16 = _prologue(x, norm_w[0])
        hn = hn.reshape(b * l, d_model)
        qkv = [_proj(h_g.reshape(b * l, d_model), w, zero_b, mode="qkv", tn=ATT_W, n=3 * ATT_W, col0=g,
                     col_step=N_GROUPS, out_dtype=BF16, tm=2048) for g, h_g in enumerate((hn, hn4, hn16))]
        gatt = _proj(hn, w, zero_b, mode="silu", tn=ATT_W, n=ATT_W, col0=offs[3] // ATT_W, out_dtype=F32)
        zs = _proj(hn, w_z, zero_b, mode="silu", tn=wide_tile, n=d_inner, out_dtype=F32)
        xbc = _proj(hn, w_xbc, zero_b, mode="raw", tn=wide_tile, n=conv_ch, out_dtype=F32)
        dt = _proj(hn, w_dt, dt_b, mode="softplus", tn=2 * ssm_heads, n=2 * ssm_heads, out_dtype=F32)
        gates = _proj(hn, w_gates, b_gate[0].reshape(1, -1), mode="sigmoid", tn=wide_tile, n=N_BRANCH * d_model,
                      out_dtype=F32)
        att = _attention(qkv[0], qkv[1], qkv[2], gatt, b, l).reshape(b * l, ATT_W)
        xbc = _conv(xbc.reshape(b, l, conv_ch), conv_w[0], conv_b, b, l, 0, conv_ch, BF16)
        dt = dt.reshape(b, l, 2 * ssm_heads)
        yf = _ssd_pass(xbc, dt, a_log_row, None, b, l, d_inner, backward=False)
        post = (yf, zs.reshape(b, l, d_inner), d_row, ssm_norm_w[0].reshape(1, -1))
        ssm_in, ssm_r = _ssd_pass(xbc, dt, a_log_row, post, b, l, d_inner, backward=True)
        merged = _merge(att, ssm_in.reshape(b * l, d_inner), ssm_r.reshape(b * l, LANES), wa, ws, gates)
        out = _out_proj(merged, wo, x.reshape(b * l, d_model), final_norm_w.reshape(1, -1))
        outs.append(out.reshape(b, l, d_model))
    return tuple(outs)
```

```python
import functools
import math

import numpy as np
import jax
import jax.numpy as jnp
from jax import lax
from jax.experimental import pallas as pl
from jax.experimental.pallas import tpu as pltpu

F32 = jnp.float32
BF16 = jnp.bfloat16

NORM_EPS = 1e-5
ATT_GROUPS = ((128, 1), (512, 4), (2048, 16))
N_GROUPS = 3
HEADS_PER_GROUP = 12
HEAD_DIM = 128
ATT_HEADS = N_GROUPS * HEADS_PER_GROUP
ATT_W = HEADS_PER_GROUP * HEAD_DIM
QKV_W = ATT_HEADS * HEAD_DIM
ALIBI_MAX_EXP = 8.0
HALF = 64
SSM_HEAD_DIM = 64
SSM_GROUPS = 8
D_STATE = 128
D_CONV = 5
N_BRANCH = 2

VMEM_LIMIT_BYTES = 56 * 1024 * 1024
LANES = 128
PERM_BLOCK = 256
ATT_BLOCK = 2048
SUB = 128
NEG_BIG = -1e30


def _cparams(*sem):
    return pltpu.CompilerParams(dimension_semantics=sem, vmem_limit_bytes=VMEM_LIMIT_BYTES)


def _perm_matrix(d):
    n = PERM_BLOCK // d
    p = np.zeros((PERM_BLOCK, PERM_BLOCK), np.float32)
    for r in range(d):
        for i in range(n):
            p[r * n + i, i * d + r] = 1.0
    return jnp.asarray(p, BF16)


def _prologue_kernel(x_ref, w_ref, p4_ref, p16_ref, nat_ref, o4_ref, o16_ref):
    x = x_ref[...]
    y = x * lax.rsqrt(jnp.mean(x * x, axis=-1, keepdims=True) + NORM_EPS)
    yb = (y * w_ref[...]).astype(BF16)
    nat_ref[...] = yb
    y4 = jnp.dot(p4_ref[...], yb, preferred_element_type=F32).astype(BF16)
    n4 = PERM_BLOCK // 4
    for r in range(4):
        o4_ref[r] = y4[r * n4:(r + 1) * n4]
    y16 = jnp.dot(p16_ref[...], yb, preferred_element_type=F32).astype(BF16)
    n16 = PERM_BLOCK // 16
    for r in range(16):
        o16_ref[r] = y16[r * n16:(r + 1) * n16]


def _prologue(x, norm_w):
    b, l, d = x.shape
    nblk = l // PERM_BLOCK
    return pl.pallas_call(
        _prologue_kernel,
        grid=(b, nblk),
        in_specs=[
            pl.BlockSpec((None, PERM_BLOCK, d), lambda bi, i: (bi, i, 0)),
            pl.BlockSpec((1, d), lambda bi, i: (0, 0)),
            pl.BlockSpec((PERM_BLOCK, PERM_BLOCK), lambda bi, i: (0, 0)),
            pl.BlockSpec((PERM_BLOCK, PERM_BLOCK), lambda bi, i: (0, 0)),
        ],
        out_specs=[
            pl.BlockSpec((None, PERM_BLOCK, d), lambda bi, i: (bi, i, 0)),
            pl.BlockSpec((None, 4, PERM_BLOCK // 4, d), lambda bi, i: (bi, 0, i, 0)),
            pl.BlockSpec((None, 16, PERM_BLOCK // 16, d), lambda bi, i: (bi, 0, i, 0)),
        ],
        out_shape=[
            jax.ShapeDtypeStruct((b, l, d), BF16),
            jax.ShapeDtypeStruct((b, 4, l // 4, d), BF16),
            jax.ShapeDtypeStruct((b, 16, l // 16, d), BF16),
        ],
        compiler_params=_cparams("parallel", "parallel"),
        name="prologue_norm_permute",
    )(x, norm_w.reshape(1, d), _perm_matrix(4), _perm_matrix(16))


def _proj_kernel(x_ref, w_ref, b_ref, o_ref, *, mode):
    acc = jnp.dot(x_ref[...], w_ref[...], preferred_element_type=F32)
    if mode == "qkv":
        scale = jnp.where(pl.program_id(0) == 0, HEAD_DIM ** -0.5, 1.0).astype(F32)
        acc = acc * scale
    elif mode == "silu":
        acc = acc * jax.nn.sigmoid(acc)
    elif mode == "softplus":
        acc = jax.nn.softplus(acc + b_ref[...])
    elif mode == "sigmoid":
        acc = jax.nn.sigmoid(acc + b_ref[...])
    o_ref[...] = acc.astype(o_ref.dtype)


def _proj(x, w, bias, *, mode, tn, n, out_dtype, col0=0, col_step=1, tm=1024):
    m, k = x.shape
    tm = min(tm, m)
    nj, ni = n // tn, m // tm
    if mode == "qkv":
        out_shape = jax.ShapeDtypeStruct((nj, m, tn), out_dtype)
        out_spec = pl.BlockSpec((None, tm, tn), lambda j, i: (j, i, 0))
    else:
        out_shape = jax.ShapeDtypeStruct((m, n), out_dtype)
        out_spec = pl.BlockSpec((tm, tn), lambda j, i: (i, j))
    return pl.pallas_call(
        functools.partial(_proj_kernel, mode=mode),
        grid=(nj, ni),
        in_specs=[
            pl.BlockSpec((tm, k), lambda j, i: (i, 0)),
            pl.BlockSpec((k, tn), lambda j, i: (0, col0 + col_step * j)),
            pl.BlockSpec((1, tn), lambda j, i: (0, j)),
        ],
        out_specs=out_spec,
        out_shape=out_shape,
        compiler_params=_cparams("parallel", "parallel"),
        name="in_proj_" + mode,
    )(x, w, bias)


CONV_HALO = 16
CONV_CHUNK = 256


def _proj_conv_kernel(xp_ref, xc_ref, xn_ref, w_ref, cw_ref, cb_ref, o_ref, *, tiles_per_seq):
    tm = xc_ref.shape[0]
    i = pl.program_id(1)
    pos = lax.rem(i, tiles_per_seq)
    lhs = jnp.concatenate([xp_ref[...], xc_ref[...], xn_ref[...]], axis=0)
    acc = jnp.dot(lhs, w_ref[...], preferred_element_type=F32)
    keep_lo = jnp.where(pos == 0, 0.0, 1.0)
    keep_hi = jnp.where(pos == tiles_per_seq - 1, 0.0, 1.0)
    w = cw_ref[...]
    half = D_CONV // 2
    pad = 8
    n_ext = CONV_CHUNK + 2 * pad
    for c in range(tm // CONV_CHUNK):
        lo = CONV_HALO + c * CONV_CHUNK - pad
        ext = acc[lo:lo + n_ext]
        if c == 0:
            ext = jnp.concatenate([ext[:pad] * keep_lo, ext[pad:]], axis=0)
        if c == tm // CONV_CHUNK - 1:
            ext = jnp.concatenate([ext[:n_ext - pad], ext[n_ext - pad:] * keep_hi], axis=0)
        out = cb_ref[...]
        for k in range(D_CONV):
            shift = (half - k) % n_ext
            tap = ext if shift == 0 else pltpu.roll(ext, shift, axis=0)
            out = out + tap[pad:pad + CONV_CHUNK] * w[k:k + 1]
        o_ref[c * CONV_CHUNK:(c + 1) * CONV_CHUNK, :] = (out * jax.nn.sigmoid(out)).astype(o_ref.dtype)


def _proj_conv(x, w, conv_w, conv_b, l, *, tn, tm=1024):
    m, k = x.shape
    n = w.shape[1]
    hb = tm // CONV_HALO
    last = m // CONV_HALO - 1
    return pl.pallas_call(
        functools.partial(_proj_conv_kernel, tiles_per_seq=l // tm),
        grid=(n // tn, m // tm),
        in_specs=[
            pl.BlockSpec((CONV_HALO, k), lambda j, i: (jnp.maximum(i * hb - 1, 0), 0)),
            pl.BlockSpec((tm, k), lambda j, i: (i, 0)),
            pl.BlockSpec((CONV_HALO, k), lambda j, i: (jnp.minimum((i + 1) * hb, last), 0)),
            pl.BlockSpec((k, tn), lambda j, i: (0, j)),
            pl.BlockSpec((D_CONV, tn), lambda j, i: (0, j)),
            pl.BlockSpec((1, tn), lambda j, i: (0, j)),
        ],
        out_specs=pl.BlockSpec((tm, tn), lambda j, i: (i, j)),
        out_shape=jax.ShapeDtypeStruct((m, n), BF16),
        compiler_params=_cparams("parallel", "parallel"),
        name="in_proj_conv",
    )(x, x, x, w, conv_w, conv_b)


def _attn_kernel(sl_ref, rel_ref,
                 q0, k0p, k0c, k0n, v0p, v0c, v0n,
                 q1, k1p, k1c, k1n, v1p, v1c, v1n,
                 q2, k2p, k2c, k2n, v2p, v2c, v2n,
                 gate_ref, o_ref, acc_s, mb_s, lb_s):
    i = pl.program_id(1)
    h = pl.program_id(2)
    is_first = i == 0
    is_last = i == pl.num_programs(1) - 1
    ones = jnp.ones((2 * SUB, HEAD_DIM), BF16)

    def window(p_ref, c_ref, n_ref, lead, j, nsub):
        lo, hi = j * SUB - HALF, (j + 1) * SUB + HALF
        parts = []
        if j == 0:
            parts.append(p_ref[lead + (slice(None), slice(None))])
            lo = 0
        own_hi = min(hi, nsub * SUB)
        parts.append(c_ref[lead + (pl.ds(lo, own_hi - lo), slice(None))])
        if j == nsub - 1:
            parts.append(n_ref[lead + (slice(None), slice(None))])
        return parts[0] if len(parts) == 1 else jnp.concatenate(parts, axis=0)

    absrel = rel_ref[...]
    col = lax.broadcasted_iota(jnp.int32, (SUB, 2 * SUB), 1)
    in_window = absrel <= float(HALF)
    mask_lo = jnp.logical_and(is_first, col < HALF)
    mask_hi = jnp.logical_and(is_last, col >= HALF + SUB)

    def bias(g):
        dist = absrel * float(ATT_GROUPS[g][1])
        return jnp.where(in_window, -(sl_ref[g, h] * dist), NEG_BIG)

    def attend(g, q, k, v, nb, rows):
        s = lax.dot_general(q, k, (((1,), (1,)), ((), ())), preferred_element_type=F32) + nb
        m = jnp.max(s, axis=-1, keepdims=True)
        p = jnp.exp(s - m).astype(BF16)
        al = jnp.dot(p, jnp.concatenate([v, ones], axis=1), preferred_element_type=F32)
        acc_s[g, rows, :] = al[:, :HEAD_DIM]
        lb_s[g, rows, :] = al[:, HEAD_DIM:]
        mb_s[g, rows, :] = jnp.broadcast_to(m, (SUB, HEAD_DIM))

    def group(g, q_ref, kp, kc, kn, vp, vc, vn, lead, r, nsub):
        d = ATT_GROUPS[g][1]
        nb = nbs[g]
        for j in range(nsub):
            nbj = nb
            if j == 0:
                nbj = jnp.where(mask_lo, NEG_BIG, nbj)
            if j == nsub - 1:
                nbj = jnp.where(mask_hi, NEG_BIG, nbj)
            q = q_ref[lead + (pl.ds(j * SUB, SUB), slice(None))]
            rows = pl.ds(j * SUB, SUB) if d == 1 else pl.ds(j * SUB * d + r, SUB, stride=d)
            attend(g, q, window(kp, kc, kn, lead, j, nsub), window(vp, vc, vn, lead, j, nsub), nbj, rows)

    nbs = [bias(g) for g in range(N_GROUPS)]
    group(0, q0, k0p, k0c, k0n, v0p, v0c, v0n, (), 0, ATT_BLOCK // SUB)

    def g1_body(it, c):
        for rr in range(2):
            r = it * 2 + rr
            group(1, q1, k1p, k1c, k1n, v1p, v1c, v1n, (r,), r, ATT_BLOCK // 4 // SUB)
        return c

    lax.fori_loop(0, 2, g1_body, 0)

    def g2_body(it, c):
        for rr in range(8):
            r = it * 8 + rr
            group(2, q2, k2p, k2c, k2n, v2p, v2c, v2n, (r,), r, 1)
        return c

    lax.fori_loop(0, 2, g2_body, 0)

    chunk = 256

    def merge(cix, c):
        rows = pl.ds(pl.multiple_of(cix * chunk, chunk), chunk)
        m0, m1, m2 = mb_s[0, rows, :], mb_s[1, rows, :], mb_s[2, rows, :]
        mm = jnp.maximum(jnp.maximum(m0, m1), m2)
        w0, w1, w2 = jnp.exp(m0 - mm), jnp.exp(m1 - mm), jnp.exp(m2 - mm)
        num = w0 * acc_s[0, rows, :] + w1 * acc_s[1, rows, :] + w2 * acc_s[2, rows, :]
        den = w0 * lb_s[0, rows, :] + w1 * lb_s[1, rows, :] + w2 * lb_s[2, rows, :]
        o_ref[rows, :] = (num / den * gate_ref[rows, :]).astype(o_ref.dtype)
        return c

    lax.fori_loop(0, ATT_BLOCK // chunk, merge, 0)


def _alibi_table():
    hh = np.arange(1, ATT_HEADS + 1, dtype=np.float32)
    return jnp.exp2(-ALIBI_MAX_EXP * jnp.asarray(hh) / ATT_HEADS).reshape(N_GROUPS, HEADS_PER_GROUP)


def _rel_table():
    qi = np.arange(SUB)[:, None]
    kj = np.arange(2 * SUB)[None, :] - HALF
    return jnp.asarray(np.abs(qi - kj), F32)


def _attention(qkv0, qkv1, qkv2, gate, b, l):
    nblk = l // ATT_BLOCK
    qkv0 = qkv0.reshape(3, b, l, ATT_W)
    qkv1 = qkv1.reshape(3, b, 4, l // 4, ATT_W)
    qkv2 = qkv2.reshape(3, b, 16, l // 16, ATT_W)
    gate = gate.reshape(b, l, ATT_W)

    def specs(d):
        own = ATT_BLOCK // d
        hb = own // HALF
        nh = l // d // HALF
        lead = (None, None) if d == 1 else (None, None, d)
        zero = () if d == 1 else (0,)

        def mk(rows, which, fn):
            return pl.BlockSpec(lead + (rows, HEAD_DIM), lambda bi, i, h: (which, bi) + zero + (fn(i), h))

        cur = lambda i: i
        prv = lambda i: jnp.maximum(i * hb - 1, 0)
        nxt = lambda i: jnp.minimum((i + 1) * hb, nh - 1)
        return [mk(own, 0, cur),
                mk(HALF, 1, prv), mk(own, 1, cur), mk(HALF, 1, nxt),
                mk(HALF, 2, prv), mk(own, 2, cur), mk(HALF, 2, nxt)]

    in_specs = [pl.BlockSpec(memory_space=pltpu.SMEM),
                pl.BlockSpec((SUB, 2 * SUB), lambda bi, i, h: (0, 0))]
    in_specs += specs(1) + specs(4) + specs(16)
    in_specs += [pl.BlockSpec((None, ATT_BLOCK, HEAD_DIM), lambda bi, i, h: (bi, i, h))]
    return pl.pallas_call(
        _attn_kernel,
        grid=(b, nblk, HEADS_PER_GROUP),
        in_specs=in_specs,
        out_specs=pl.BlockSpec((None, ATT_BLOCK, HEAD_DIM), lambda bi, i, h: (bi, i, h)),
        out_shape=jax.ShapeDtypeStruct((b, l, ATT_W), BF16),
        scratch_shapes=[pltpu.VMEM((N_GROUPS, ATT_BLOCK, HEAD_DIM), F32)] * 3,
        compiler_params=_cparams("parallel", "parallel", "arbitrary"),
        name="dilated_attention",
    )(_alibi_table(), _rel_table(), *([qkv0] * 7), *([qkv1] * 7), *([qkv2] * 7), gate)


SSD_Q = 128
SSD_BLOCK = 256


def _bf16_terms(v, n):
    out = []
    for _ in range(n):
        t = v.astype(BF16).astype(F32)
        out.append(t)
        v = v - t
    return out


def _prefix_sum_rows(x):
    n = x.shape[0]
    row = lax.broadcasted_iota(jnp.int32, x.shape, 0)
    k = 1
    while k < n:
        x = x + jnp.where(row >= k, pltpu.roll(x, k, axis=0), 0.0)
        k *= 2
    return x


def _ssd_kernel(*refs, heads, groups, backward):
    if backward:
        x_ref, b_ref, c_ref, dt_ref, a_ref, e_ref, yf_ref, zs_ref, d_ref, w_ref, o_ref, r_ref, st_ref = refs
    else:
        x_ref, b_ref, c_ref, dt_ref, a_ref, e_ref, o_ref, st_ref = refs
    q = SSD_Q
    p = SSM_HEAD_DIM
    n = D_STATE
    hpg = heads // groups
    nc = x_ref.shape[0] // q

    @pl.when(pl.program_id(1) == 0)
    def _():
        st_ref[...] = jnp.zeros(st_ref.shape, F32)

    row = lax.broadcasted_iota(jnp.int32, (q, q), 0)
    col = lax.broadcasted_iota(jnp.int32, (q, q), 1)
    lower, upper = row >= col, row <= col
    is_fwd_lane = lax.broadcasted_iota(jnp.int32, (q, 2 * heads), 1) < heads
    a_all = -jnp.exp(a_ref[...])

    def split(v):
        return jnp.concatenate(_bf16_terms(v, 2), axis=1).astype(BF16)

    log2e = math.log2(math.e)
    lane2 = lax.broadcasted_iota(jnp.int32, (q, 2 * p), 1)

    def chunk(c, carry):
        cc = nc - 1 - c if backward else c
        rows = pl.ds(pl.multiple_of(cc * q, q), q)
        dt = dt_ref[rows, :]
        da = dt * a_all
        pre = _prefix_sum_rows(da)
        tot = pre[q - 1:q, :]
        acs_all = jnp.where(is_fwd_lane, pre, tot - pre + da)
        wx_all = jnp.dot(split(dt * jnp.exp(tot - acs_all)), e_ref[...],
                         preferred_element_type=F32)
        sc_all = jnp.dot(split(jnp.exp(acs_all)), e_ref[...],
                         preferred_element_type=F32)
        ssq = jnp.zeros((q, 1), F32)
        if not backward:
            as2 = acs_all * log2e
            bs2 = (acs_all - jnp.log(dt)) * log2e
            bs_t = bs2.T
            cbs = [lax.dot_general(c_ref[rows, g * n:(g + 1) * n], b_ref[rows, g * n:(g + 1) * n],
                                   (((1,), (1,)), ((), ())), preferred_element_type=F32) for g in range(groups)]
        for g in range(groups):
            cols = slice(g * hpg * p, (g + 1) * hpg * p)
            bg = b_ref[rows, g * n:(g + 1) * n]
            cg = c_ref[rows, g * n:(g + 1) * n]
            xg = x_ref[rows, cols]
            sc = sc_all[:, cols]
            wx = wx_all[:, cols]
            cdx = sc[0:1, :] if backward else sc[q - 1:q, :]
            st = st_ref[g]
            yg = jnp.dot(cg, st.astype(BF16), preferred_element_type=F32) * sc
            if not backward:
                for hh in range(0, hpg, 2):
                    ls = []
                    for h in (g * hpg + hh, g * hpg + hh + 1):
                        hb = heads + h
                        ef = jnp.exp2(as2[:, h:h + 1] - bs_t[h:h + 1, :])
                        eb = jnp.exp2(as2[:, hb:hb + 1] - bs_t[hb:hb + 1, :])
                        dec = jnp.where(lower, ef, 0.0) + jnp.where(upper, eb, 0.0)
                        ls.append((dec * cbs[g]).astype(BF16))
                    pair = xg[:, hh * p:(hh + 2) * p]
                    zero = jnp.zeros_like(pair)
                    rhs = jnp.concatenate([jnp.where(lane2 < p, pair, zero), jnp.where(lane2 >= p, pair, zero)],
                                          axis=0)
                    yd = jnp.dot(jnp.concatenate(ls, axis=1), rhs, preferred_element_type=F32)
                    pc = slice(hh * p, (hh + 2) * p)
                    o_ref[rows, g * hpg * p + hh * p:g * hpg * p + (hh + 2) * p] = yg[:, pc] + yd
            xf = xg.astype(F32)
            if backward:
                yg = (yg + yf_ref[rows, cols] + d_ref[:, cols] * xf) * zs_ref[rows, cols]
                ssq = ssq + jnp.sum(yg * yg, axis=-1, keepdims=True)
                o_ref[rows, cols] = (yg * w_ref[:, cols]).astype(o_ref.dtype)
            xdec = (xf * wx).astype(BF16)
            new = lax.dot_general(bg, xdec, (((0,), (0,)), ((), ())), preferred_element_type=F32)
            st_ref[g] = st * cdx + new
        if backward:
            inv = lax.rsqrt(ssq * (1.0 / (heads * p)) + NORM_EPS)
            r_ref[rows, :] = jnp.broadcast_to(inv, (q, r_ref.shape[-1]))
        return carry

    lax.fori_loop(0, nc, chunk, 0)


def _ssd_pass(xbc, dt, a_log_row, post, b, l, hp, backward):
    gn = SSM_GROUPS * D_STATE
    assert hp % gn == 0
    heads = hp // SSM_HEAD_DIM
    nblk = l // SSD_BLOCK
    e = np.zeros((4 * heads, hp), np.float32)
    off = heads if backward else 0
    for h in range(heads):
        e[off + h, h * SSM_HEAD_DIM:(h + 1) * SSM_HEAD_DIM] = 1.0
        e[2 * heads + off + h, h * SSM_HEAD_DIM:(h + 1) * SSM_HEAD_DIM] = 1.0
    const = lambda bi, i: (0, 0)

    def tok(w, cblk=0):
        return pl.BlockSpec((None, SSD_BLOCK, w), lambda bi, i: (bi, nblk - 1 - i if backward else i, cblk))

    in_specs = [tok(hp), tok(gn, hp // gn), tok(gn, hp // gn + 1), tok(2 * heads),
                pl.BlockSpec((1, 2 * heads), const), pl.BlockSpec((4 * heads, hp), const)]
    args = [xbc, xbc, xbc, dt, a_log_row, jnp.asarray(e, BF16)]
    scratch = [pltpu.VMEM((SSM_GROUPS, D_STATE, hp // SSM_GROUPS), F32)]
    if backward:
        in_specs += [tok(hp), tok(hp), pl.BlockSpec((1, hp), const), pl.BlockSpec((1, hp), const)]
        args += list(post)
        out_specs = [tok(hp), tok(LANES)]
        out_shape = [jax.ShapeDtypeStruct((b, l, hp), BF16), jax.ShapeDtypeStruct((b, l, LANES), F32)]
    else:
        out_specs = tok(hp)
        out_shape = jax.ShapeDtypeStruct((b, l, hp), F32)
    return pl.pallas_call(
        functools.partial(_ssd_kernel, heads=heads, groups=SSM_GROUPS, backward=backward),
        grid=(b, nblk),
        in_specs=in_specs,
        out_specs=out_specs,
        out_shape=out_shape,
        scratch_shapes=scratch,
        compiler_params=_cparams("parallel", "arbitrary"),
        name="ssd_scan_bwd" if backward else "ssd_scan_fwd",
    )(*args)


def _merge_kernel(att_ref, ssm_ref, r_ref, wa_ref, ws_ref, ga_ref, gs_ref, o_ref):
    a = jnp.dot(att_ref[...], wa_ref[...], preferred_element_type=F32)
    s = jnp.dot(ssm_ref[...], ws_ref[...], preferred_element_type=F32) * r_ref[:, 0:1]
    o_ref[...] = (ga_ref[...] * a + gs_ref[...] * s).astype(o_ref.dtype)


def _merge(att, ssm_in, ssm_r, wa, ws, gates, tm=512, tn=1024):
    m = att.shape[0]
    n = wa.shape[1]
    nj = n // tn
    return pl.pallas_call(
        _merge_kernel,
        grid=(nj, m // tm),
        in_specs=[
            pl.BlockSpec((tm, att.shape[1]), lambda j, i: (i, 0)),
            pl.BlockSpec((tm, ssm_in.shape[1]), lambda j, i: (i, 0)),
            pl.BlockSpec((tm, LANES), lambda j, i: (i, 0)),
            pl.BlockSpec((wa.shape[0], tn), lambda j, i: (0, j)),
            pl.BlockSpec((ws.shape[0], tn), lambda j, i: (0, j)),
            pl.BlockSpec((tm, tn), lambda j, i: (i, j)),
            pl.BlockSpec((tm, tn), lambda j, i: (i, nj + j)),
        ],
        out_specs=pl.BlockSpec((tm, tn), lambda j, i: (i, j)),
        out_shape=jax.ShapeDtypeStruct((m, n), BF16),
        compiler_params=_cparams("parallel", "parallel"),
        name="branch_merge",
    )(att, ssm_in, ssm_r, wa, ws, gates, gates)


def _out_kernel(m_ref, w_ref, x_ref, nw_ref, o_ref):
    y = x_ref[...] + jnp.dot(m_ref[...], w_ref[...], preferred_element_type=F32)
    y = y * lax.rsqrt(jnp.mean(y * y, axis=-1, keepdims=True) + NORM_EPS)
    o_ref[...] = y * nw_ref[...]


def _out_proj(merged, w, x, nw_row, tm=512):
    m, n = x.shape
    return pl.pallas_call(
        _out_kernel,
        grid=(m // tm,),
        in_specs=[
            pl.BlockSpec((tm, merged.shape[1]), lambda i: (i, 0)),
            pl.BlockSpec(w.shape, lambda i: (0, 0)),
            pl.BlockSpec((tm, n), lambda i: (i, 0)),
            pl.BlockSpec((1, n), lambda i: (0, 0)),
        ],
        out_specs=pl.BlockSpec((tm, n), lambda i: (i, 0)),
        out_shape=jax.ShapeDtypeStruct((m, n), F32),
        compiler_params=_cparams("parallel"),
        name="out_proj_norm",
    )(merged, w, x, nw_row)


def kernel(x_prompt, x_sample, norm_w, w_in, b_gate, conv_w, conv_b, dt_bias, a_log, d_skip, ssm_norm_w, w_attn_o, w_ssm_o, w_out, final_norm_w):
    d_model = x_prompt.shape[-1]
    d_inner = ssm_norm_w.shape[-1]
    conv_ch = conv_w.shape[-1]
    ssm_heads = d_skip.shape[-1]
    offs = np.cumsum([0, QKV_W, QKV_W, QKV_W, ATT_W, d_inner, conv_ch, 2 * ssm_heads, N_BRANCH * d_model])
    w = w_in[0].astype(BF16)
    w_dt = w[:, offs[6]:offs[7]]
    dt_b = dt_bias[0].astype(F32).reshape(1, -1)
    w_z, w_xbc, w_gates = w[:, offs[4]:offs[5]], w[:, offs[5]:offs[6]], w[:, offs[7]:offs[8]]
    wide_tile = 2048
    zero_b = jnp.zeros((1, conv_ch), F32)
    wa, ws, wo = w_attn_o[0].astype(BF16), w_ssm_o[0].astype(BF16), w_out[0].astype(BF16)
    a_log_row = a_log[0].astype(F32).reshape(1, -1)
    d_row = jnp.repeat(d_skip[0].astype(F32), SSM_HEAD_DIM).reshape(1, -1)
    conv_b = conv_b[0].reshape(1, -1)
    assert offs[3] % ATT_W == 0

    outs = []
    for x in (x_prompt, x_sample):
        b, l, _ = x.shape
        hn, hn4, hn16 = _prologue(x, norm_w[0])
        hn = hn.reshape(b * l, d_model)
        qkv = [_proj(h_g.reshape(b * l, d_model), w, zero_b, mode="qkv", tn=ATT_W, n=3 * ATT_W, col0=g,
                     col_step=N_GROUPS, out_dtype=BF16, tm=2048) for g, h_g in enumerate((hn, hn4, hn16))]
        gatt = _proj(hn, w, zero_b, mode="silu", tn=ATT_W, n=ATT_W, col0=offs[3] // ATT_W, out_dtype=F32)
        zs = _proj(hn, w_z, zero_b, mode="silu", tn=wide_tile, n=d_inner, out_dtype=F32)
        xbc = _proj_conv(hn, w_xbc, conv_w[0], conv_b, l, tn=1024).reshape(b, l, conv_ch)
        dt = _proj(hn, w_dt, dt_b, mode="softplus", tn=2 * ssm_heads, n=2 * ssm_heads, out_dtype=F32)
        gates = _proj(hn, w_gates, b_gate[0].reshape(1, -1), mode="sigmoid", tn=wide_tile, n=N_BRANCH * d_model,
                      out_dtype=F32)
        att = _attention(qkv[0], qkv[1], qkv[2], gatt, b, l).reshape(b * l, ATT_W)
        dt = dt.reshape(b, l, 2 * ssm_heads)
        yf = _ssd_pass(xbc, dt, a_log_row, None, b, l, d_inner, backward=False)
        post = (yf, zs.reshape(b, l, d_inner), d_row, ssm_norm_w[0].reshape(1, -1))
        ssm_in, ssm_r = _ssd_pass(xbc, dt, a_log_row, post, b, l, d_inner, backward=True)
        merged = _merge(att, ssm_in.reshape(b * l, d_inner), ssm_r.reshape(b * l, LANES), wa, ws, gates)
        out = _out_proj(merged, wo, x.reshape(b * l, d_model), final_norm_w.reshape(1, -1))
        outs.append(out.reshape(b, l, d_model))
    return tuple(outs)
```

```python
import functools
import math

import numpy as np
import jax
import jax.numpy as jnp
from jax import lax
from jax.experimental import pallas as pl
from jax.experimental.pallas import tpu as pltpu

F32 = jnp.float32
BF16 = jnp.bfloat16

NORM_EPS = 1e-5
ATT_GROUPS = ((128, 1), (512, 4), (2048, 16))
N_GROUPS = 3
HEADS_PER_GROUP = 12
HEAD_DIM = 128
ATT_HEADS = N_GROUPS * HEADS_PER_GROUP
ATT_W = HEADS_PER_GROUP * HEAD_DIM
QKV_W = ATT_HEADS * HEAD_DIM
ALIBI_MAX_EXP = 8.0
HALF = 64
SSM_HEAD_DIM = 64
SSM_GROUPS = 8
D_STATE = 128
D_CONV = 5
N_BRANCH = 2

VMEM_LIMIT_BYTES = 56 * 1024 * 1024
LANES = 128
PERM_BLOCK = 256
ATT_BLOCK = 2048
SUB = 128
NEG_BIG = -1e30


def _cparams(*sem):
    return pltpu.CompilerParams(dimension_semantics=sem, vmem_limit_bytes=VMEM_LIMIT_BYTES)


def _perm_matrix(d):
    n = PERM_BLOCK // d
    p = np.zeros((PERM_BLOCK, PERM_BLOCK), np.float32)
    for r in range(d):
        for i in range(n):
            p[r * n + i, i * d + r] = 1.0
    return jnp.asarray(p, BF16)


def _prologue_kernel(x_ref, w_ref, p4_ref, p16_ref, nat_ref, o4_ref, o16_ref):
    x = x_ref[...]
    y = x * lax.rsqrt(jnp.mean(x * x, axis=-1, keepdims=True) + NORM_EPS)
    yb = (y * w_ref[...]).astype(BF16)
    nat_ref[...] = yb
    y4 = jnp.dot(p4_ref[...], yb, preferred_element_type=F32).astype(BF16)
    n4 = PERM_BLOCK // 4
    for r in range(4):
        o4_ref[r] = y4[r * n4:(r + 1) * n4]
    y16 = jnp.dot(p16_ref[...], yb, preferred_element_type=F32).astype(BF16)
    n16 = PERM_BLOCK // 16
    for r in range(16):
        o16_ref[r] = y16[r * n16:(r + 1) * n16]


def _prologue(x, norm_w):
    b, l, d = x.shape
    nblk = l // PERM_BLOCK
    return pl.pallas_call(
        _prologue_kernel,
        grid=(b, nblk),
        in_specs=[
            pl.BlockSpec((None, PERM_BLOCK, d), lambda bi, i: (bi, i, 0)),
            pl.BlockSpec((1, d), lambda bi, i: (0, 0)),
            pl.BlockSpec((PERM_BLOCK, PERM_BLOCK), lambda bi, i: (0, 0)),
            pl.BlockSpec((PERM_BLOCK, PERM_BLOCK), lambda bi, i: (0, 0)),
        ],
        out_specs=[
            pl.BlockSpec((None, PERM_BLOCK, d), lambda bi, i: (bi, i, 0)),
            pl.BlockSpec((None, 4, PERM_BLOCK // 4, d), lambda bi, i: (bi, 0, i, 0)),
            pl.BlockSpec((None, 16, PERM_BLOCK // 16, d), lambda bi, i: (bi, 0, i, 0)),
        ],
        out_shape=[
            jax.ShapeDtypeStruct((b, l, d), BF16),
            jax.ShapeDtypeStruct((b, 4, l // 4, d), BF16),
            jax.ShapeDtypeStruct((b, 16, l // 16, d), BF16),
        ],
        compiler_params=_cparams("parallel", "parallel"),
        name="prologue_norm_permute",
    )(x, norm_w.reshape(1, d), _perm_matrix(4), _perm_matrix(16))


def _proj_kernel(x_ref, w_ref, b_ref, o_ref, *, mode):
    acc = jnp.dot(x_ref[...], w_ref[...], preferred_element_type=F32)
    if mode == "qkv":
        scale = jnp.where(pl.program_id(0) == 0, HEAD_DIM ** -0.5, 1.0).astype(F32)
        acc = acc * scale
    elif mode == "silu":
        acc = acc * jax.nn.sigmoid(acc)
    elif mode == "softplus":
        acc = jax.nn.softplus(acc + b_ref[...])
    elif mode == "sigmoid":
        acc = jax.nn.sigmoid(acc + b_ref[...])
    o_ref[...] = acc.astype(o_ref.dtype)


def _proj(x, w, bias, *, mode, tn, n, out_dtype, col0=0, col_step=1, tm=1024):
    m, k = x.shape
    tm = min(tm, m)
    nj, ni = n // tn, m // tm
    if mode == "qkv":
        out_shape = jax.ShapeDtypeStruct((nj, m, tn), out_dtype)
        out_spec = pl.BlockSpec((None, tm, tn), lambda j, i: (j, i, 0))
    else:
        out_shape = jax.ShapeDtypeStruct((m, n), out_dtype)
        out_spec = pl.BlockSpec((tm, tn), lambda j, i: (i, j))
    return pl.pallas_call(
        functools.partial(_proj_kernel, mode=mode),
        grid=(nj, ni),
        in_specs=[
            pl.BlockSpec((tm, k), lambda j, i: (i, 0)),
            pl.BlockSpec((k, tn), lambda j, i: (0, col0 + col_step * j)),
            pl.BlockSpec((1, tn), lambda j, i: (0, j)),
        ],
        out_specs=out_spec,
        out_shape=out_shape,
        compiler_params=_cparams("parallel", "parallel"),
        name="in_proj_" + mode,
    )(x, w, bias)


CONV_HALO = 16
CONV_CHUNK = 256


def _proj_conv_kernel(xp_ref, xc_ref, xn_ref, w_ref, cw_ref, cb_ref, o_ref, *, tiles_per_seq):
    tm = xc_ref.shape[0]
    i = pl.program_id(1)
    pos = lax.rem(i, tiles_per_seq)
    lhs = jnp.concatenate([xp_ref[...], xc_ref[...], xn_ref[...]], axis=0)
    acc = jnp.dot(lhs, w_ref[...], preferred_element_type=F32)
    keep_lo = jnp.where(pos == 0, 0.0, 1.0)
    keep_hi = jnp.where(pos == tiles_per_seq - 1, 0.0, 1.0)
    w = cw_ref[...]
    half = D_CONV // 2
    pad = 8
    n_ext = CONV_CHUNK + 2 * pad
    for c in range(tm // CONV_CHUNK):
        lo = CONV_HALO + c * CONV_CHUNK - pad
        ext = acc[lo:lo + n_ext]
        if c == 0:
            ext = jnp.concatenate([ext[:pad] * keep_lo, ext[pad:]], axis=0)
        if c == tm // CONV_CHUNK - 1:
            ext = jnp.concatenate([ext[:n_ext - pad], ext[n_ext - pad:] * keep_hi], axis=0)
        out = cb_ref[...]
        for k in range(D_CONV):
            shift = (half - k) % n_ext
            tap = ext if shift == 0 else pltpu.roll(ext, shift, axis=0)
            out = out + tap[pad:pad + CONV_CHUNK] * w[k:k + 1]
        o_ref[c * CONV_CHUNK:(c + 1) * CONV_CHUNK, :] = (out * jax.nn.sigmoid(out)).astype(o_ref.dtype)


def _proj_conv(x, w, conv_w, conv_b, l, *, tn, tm=1024):
    m, k = x.shape
    n = w.shape[1]
    hb = tm // CONV_HALO
    last = m // CONV_HALO - 1
    return pl.pallas_call(
        functools.partial(_proj_conv_kernel, tiles_per_seq=l // tm),
        grid=(n // tn, m // tm),
        in_specs=[
            pl.BlockSpec((CONV_HALO, k), lambda j, i: (jnp.maximum(i * hb - 1, 0), 0)),
            pl.BlockSpec((tm, k), lambda j, i: (i, 0)),
            pl.BlockSpec((CONV_HALO, k), lambda j, i: (jnp.minimum((i + 1) * hb, last), 0)),
            pl.BlockSpec((k, tn), lambda j, i: (0, j)),
            pl.BlockSpec((D_CONV, tn), lambda j, i: (0, j)),
            pl.BlockSpec((1, tn), lambda j, i: (0, j)),
        ],
        out_specs=pl.BlockSpec((tm, tn), lambda j, i: (i, j)),
        out_shape=jax.ShapeDtypeStruct((m, n), BF16),
        compiler_params=_cparams("parallel", "parallel"),
        name="in_proj_conv",
    )(x, x, x, w, conv_w, conv_b)


def _attn_kernel(sl_ref, rel_ref,
                 q0, k0p, k0c, k0n, v0p, v0c, v0n,
                 q1, k1p, k1c, k1n, v1p, v1c, v1n,
                 q2, k2p, k2c, k2n, v2p, v2c, v2n,
                 gate_ref, o_ref, acc_s, lse_s):
    i = pl.program_id(1)
    h = pl.program_id(2)
    is_first = i == 0
    is_last = i == pl.num_programs(1) - 1
    ones = jnp.ones((2 * SUB, HEAD_DIM), BF16)

    def window(p_ref, c_ref, n_ref, lead, j, nsub):
        lo, hi = j * SUB - HALF, (j + 1) * SUB + HALF
        parts = []
        if j == 0:
            parts.append(p_ref[lead + (slice(None), slice(None))])
            lo = 0
        own_hi = min(hi, nsub * SUB)
        parts.append(c_ref[lead + (pl.ds(lo, own_hi - lo), slice(None))])
        if j == nsub - 1:
            parts.append(n_ref[lead + (slice(None), slice(None))])
        return parts[0] if len(parts) == 1 else jnp.concatenate(parts, axis=0)

    absrel = rel_ref[...]
    col = lax.broadcasted_iota(jnp.int32, (SUB, 2 * SUB), 1)
    in_window = absrel <= float(HALF)
    mask_lo = jnp.logical_and(is_first, col < HALF)
    mask_hi = jnp.logical_and(is_last, col >= HALF + SUB)

    def bias(g):
        dist = absrel * float(ATT_GROUPS[g][1])
        return jnp.where(in_window, -(sl_ref[g, h] * dist), NEG_BIG)

    def attend(g, q, k, v, nb, rows):
        s = lax.dot_general(q, k, (((1,), (1,)), ((), ())), preferred_element_type=F32) + nb
        m = jnp.max(s, axis=-1, keepdims=True)
        p = jnp.exp(s - m).astype(BF16)
        al = jnp.dot(p, jnp.concatenate([v, ones], axis=1), preferred_element_type=F32)
        lsum = al[:, HEAD_DIM:]
        acc_s[g, rows, :] = al[:, :HEAD_DIM] / lsum
        lse_s[g, rows, :] = m + jnp.log(lsum)

    def group(g, q_ref, kp, kc, kn, vp, vc, vn, lead, r, nsub):
        d = ATT_GROUPS[g][1]
        nb = nbs[g]
        for j in range(nsub):
            nbj = nb
            if j == 0:
                nbj = jnp.where(mask_lo, NEG_BIG, nbj)
            if j == nsub - 1:
                nbj = jnp.where(mask_hi, NEG_BIG, nbj)
            q = q_ref[lead + (pl.ds(j * SUB, SUB), slice(None))]
            rows = pl.ds(j * SUB, SUB) if d == 1 else pl.ds(j * SUB * d + r, SUB, stride=d)
            attend(g, q, window(kp, kc, kn, lead, j, nsub), window(vp, vc, vn, lead, j, nsub), nbj, rows)

    nbs = [bias(g) for g in range(N_GROUPS)]
    group(0, q0, k0p, k0c, k0n, v0p, v0c, v0n, (), 0, ATT_BLOCK // SUB)

    for r in range(4):
        group(1, q1, k1p, k1c, k1n, v1p, v1c, v1n, (r,), r, ATT_BLOCK // 4 // SUB)

    for r in range(16):
        group(2, q2, k2p, k2c, k2n, v2p, v2c, v2n, (r,), r, 1)

    chunk = 256

    def merge(cix, c):
        rows = pl.ds(pl.multiple_of(cix * chunk, chunk), chunk)
        m0, m1, m2 = lse_s[0, rows, :], lse_s[1, rows, :], lse_s[2, rows, :]
        mm = jnp.maximum(jnp.maximum(m0, m1), m2)
        w0, w1, w2 = jnp.exp(m0 - mm), jnp.exp(m1 - mm), jnp.exp(m2 - mm)
        num = w0 * acc_s[0, rows, :] + w1 * acc_s[1, rows, :] + w2 * acc_s[2, rows, :]
        o_ref[rows, :] = (num / (w0 + w1 + w2) * gate_ref[rows, :]).astype(o_ref.dtype)
        return c

    lax.fori_loop(0, ATT_BLOCK // chunk, merge, 0)


def _alibi_table():
    hh = np.arange(1, ATT_HEADS + 1, dtype=np.float32)
    return jnp.exp2(-ALIBI_MAX_EXP * jnp.asarray(hh) / ATT_HEADS).reshape(N_GROUPS, HEADS_PER_GROUP)


def _rel_table():
    qi = np.arange(SUB)[:, None]
    kj = np.arange(2 * SUB)[None, :] - HALF
    return jnp.asarray(np.abs(qi - kj), F32)


def _attention(qkv0, qkv1, qkv2, gate, b, l):
    nblk = l // ATT_BLOCK
    qkv0 = qkv0.reshape(3, b, l, ATT_W)
    qkv1 = qkv1.reshape(3, b, 4, l // 4, ATT_W)
    qkv2 = qkv2.reshape(3, b, 16, l // 16, ATT_W)
    gate = gate.reshape(b, l, ATT_W)

    def specs(d):
        own = ATT_BLOCK // d
        hb = own // HALF
        nh = l // d // HALF
        lead = (None, None) if d == 1 else (None, None, d)
        zero = () if d == 1 else (0,)

        def mk(rows, which, fn):
            return pl.BlockSpec(lead + (rows, HEAD_DIM), lambda bi, i, h: (which, bi) + zero + (fn(i), h))

        cur = lambda i: i
        prv = lambda i: jnp.maximum(i * hb - 1, 0)
        nxt = lambda i: jnp.minimum((i + 1) * hb, nh - 1)
        return [mk(own, 0, cur),
                mk(HALF, 1, prv), mk(own, 1, cur), mk(HALF, 1, nxt),
                mk(HALF, 2, prv), mk(own, 2, cur), mk(HALF, 2, nxt)]

    in_specs = [pl.BlockSpec(memory_space=pltpu.SMEM),
                pl.BlockSpec((SUB, 2 * SUB), lambda bi, i, h: (0, 0))]
    in_specs += specs(1) + specs(4) + specs(16)
    in_specs += [pl.BlockSpec((None, ATT_BLOCK, HEAD_DIM), lambda bi, i, h: (bi, i, h))]
    return pl.pallas_call(
        _attn_kernel,
        grid=(b, nblk, HEADS_PER_GROUP),
        in_specs=in_specs,
        out_specs=pl.BlockSpec((None, ATT_BLOCK, HEAD_DIM), lambda bi, i, h: (bi, i, h)),
        out_shape=jax.ShapeDtypeStruct((b, l, ATT_W), BF16),
        scratch_shapes=[pltpu.VMEM((N_GROUPS, ATT_BLOCK, HEAD_DIM), F32)] * 2,
        compiler_params=_cparams("parallel", "parallel", "arbitrary"),
        name="dilated_attention",
    )(_alibi_table(), _rel_table(), *([qkv0] * 7), *([qkv1] * 7), *([qkv2] * 7), gate)


SSD_Q = 128
SSD_BLOCK = 256


def _bf16_terms(v, n):
    out = []
    for _ in range(n):
        t = v.astype(BF16).astype(F32)
        out.append(t)
        v = v - t
    return out


def _prefix_sum_rows(x):
    n = x.shape[0]
    row = lax.broadcasted_iota(jnp.int32, x.shape, 0)
    k = 1
    while k < n:
        x = x + jnp.where(row >= k, pltpu.roll(x, k, axis=0), 0.0)
        k *= 2
    return x


def _ssd_kernel(*refs, heads, groups, backward):
    if backward:
        x_ref, b_ref, c_ref, dt_ref, a_ref, e_ref, yf_ref, zs_ref, d_ref, w_ref, o_ref, r_ref, st_ref = refs
    else:
        x_ref, b_ref, c_ref, dt_ref, a_ref, e_ref, o_ref, st_ref = refs
    q = SSD_Q
    p = SSM_HEAD_DIM
    n = D_STATE
    hpg = heads // groups
    nc = x_ref.shape[0] // q

    @pl.when(pl.program_id(1) == 0)
    def _():
        st_ref[...] = jnp.zeros(st_ref.shape, F32)

    row = lax.broadcasted_iota(jnp.int32, (q, q), 0)
    col = lax.broadcasted_iota(jnp.int32, (q, q), 1)
    lower, upper = row >= col, row <= col
    is_fwd_lane = lax.broadcasted_iota(jnp.int32, (q, 2 * heads), 1) < heads
    a_all = -jnp.exp(a_ref[...])

    def split(v):
        return jnp.concatenate(_bf16_terms(v, 2), axis=1).astype(BF16)

    log2e = math.log2(math.e)
    lane2 = lax.broadcasted_iota(jnp.int32, (q, 2 * p), 1)

    def chunk(c, carry):
        cc = nc - 1 - c if backward else c
        rows = pl.ds(pl.multiple_of(cc * q, q), q)
        dt = dt_ref[rows, :]
        da = dt * a_all
        pre = _prefix_sum_rows(da)
        tot = pre[q - 1:q, :]
        acs_all = jnp.where(is_fwd_lane, pre, tot - pre + da)
        wx_all = jnp.dot(split(dt * jnp.exp(tot - acs_all)), e_ref[...],
                         preferred_element_type=F32)
        sc_all = jnp.dot(split(jnp.exp(acs_all)), e_ref[...],
                         preferred_element_type=F32)
        ssq = jnp.zeros((q, 1), F32)
        if not backward:
            as2 = acs_all * log2e
            bs2 = (acs_all - jnp.log(dt)) * log2e
            bs_t = bs2.T
            cbs = [lax.dot_general(c_ref[rows, g * n:(g + 1) * n], b_ref[rows, g * n:(g + 1) * n],
                                   (((1,), (1,)), ((), ())), preferred_element_type=F32) for g in range(groups)]
        for g in range(groups):
            cols = slice(g * hpg * p, (g + 1) * hpg * p)
            bg = b_ref[rows, g * n:(g + 1) * n]
            cg = c_ref[rows, g * n:(g + 1) * n]
            xg = x_ref[rows, cols]
            sc = sc_all[:, cols]
            wx = wx_all[:, cols]
            cdx = sc[0:1, :] if backward else sc[q - 1:q, :]
            st = st_ref[g]
            yg = jnp.dot(cg, st.astype(BF16), preferred_element_type=F32) * sc
            if not backward:
                for hh in range(0, hpg, 2):
                    ls = []
                    for h in (g * hpg + hh, g * hpg + hh + 1):
                        hb = heads + h
                        ef = jnp.exp2(as2[:, h:h + 1] - bs_t[h:h + 1, :])
                        eb = jnp.exp2(as2[:, hb:hb + 1] - bs_t[hb:hb + 1, :])
                        dec = jnp.where(lower, ef, 0.0) + jnp.where(upper, eb, 0.0)
                        ls.append((dec * cbs[g]).astype(BF16))
                    pair = xg[:, hh * p:(hh + 2) * p]
                    zero = jnp.zeros_like(pair)
                    rhs = jnp.concatenate([jnp.where(lane2 < p, pair, zero), jnp.where(lane2 >= p, pair, zero)],
                                          axis=0)
                    yd = jnp.dot(jnp.concatenate(ls, axis=1), rhs, preferred_element_type=F32)
                    pc = slice(hh * p, (hh + 2) * p)
                    o_ref[rows, g * hpg * p + hh * p:g * hpg * p + (hh + 2) * p] = yg[:, pc] + yd
            xf = xg.astype(F32)
            if backward:
                yg = (yg + yf_ref[rows, cols] + d_ref[:, cols] * xf) * zs_ref[rows, cols]
                ssq = ssq + jnp.sum(yg * yg, axis=-1, keepdims=True)
                o_ref[rows, cols] = (yg * w_ref[:, cols]).astype(o_ref.dtype)
            xdec = (xf * wx).astype(BF16)
            new = lax.dot_general(bg, xdec, (((0,), (0,)), ((), ())), preferred_element_type=F32)
            st_ref[g] = st * cdx + new
        if backward:
            inv = lax.rsqrt(ssq * (1.0 / (heads * p)) + NORM_EPS)
            r_ref[rows, :] = jnp.broadcast_to(inv, (q, r_ref.shape[-1]))
        return carry

    lax.fori_loop(0, nc, chunk, 0)


def _ssd_pass(xbc, dt, a_log_row, post, b, l, hp, backward):
    gn = SSM_GROUPS * D_STATE
    assert hp % gn == 0
    heads = hp // SSM_HEAD_DIM
    nblk = l // SSD_BLOCK
    e = np.zeros((4 * heads, hp), np.float32)
    off = heads if backward else 0
    for h in range(heads):
        e[off + h, h * SSM_HEAD_DIM:(h + 1) * SSM_HEAD_DIM] = 1.0
        e[2 * heads + off + h, h * SSM_HEAD_DIM:(h + 1) * SSM_HEAD_DIM] = 1.0
    const = lambda bi, i: (0, 0)

    def tok(w, cblk=0):
        return pl.BlockSpec((None, SSD_BLOCK, w), lambda bi, i: (bi, nblk - 1 - i if backward else i, cblk))

    in_specs = [tok(hp), tok(gn, hp // gn), tok(gn, hp // gn + 1), tok(2 * heads),
                pl.BlockSpec((1, 2 * heads), const), pl.BlockSpec((4 * heads, hp), const)]
    args = [xbc, xbc, xbc, dt, a_log_row, jnp.asarray(e, BF16)]
    scratch = [pltpu.VMEM((SSM_GROUPS, D_STATE, hp // SSM_GROUPS), F32)]
    if backward:
        in_specs += [tok(hp), tok(hp), pl.BlockSpec((1, hp), const), pl.BlockSpec((1, hp), const)]
        args += list(post)
        out_specs = [tok(hp), tok(LANES)]
        out_shape = [jax.ShapeDtypeStruct((b, l, hp), BF16), jax.ShapeDtypeStruct((b, l, LANES), F32)]
    else:
        out_specs = tok(hp)
        out_shape = jax.ShapeDtypeStruct((b, l, hp), F32)
    return pl.pallas_call(
        functools.partial(_ssd_kernel, heads=heads, groups=SSM_GROUPS, backward=backward),
        grid=(b, nblk),
        in_specs=in_specs,
        out_specs=out_specs,
        out_shape=out_shape,
        scratch_shapes=scratch,
        compiler_params=_cparams("parallel", "arbitrary"),
        name="ssd_scan_bwd" if backward else "ssd_scan_fwd",
    )(*args)


def _merge_kernel(att_ref, ssm_ref, r_ref, wa_ref, ws_ref, ga_ref, gs_ref, o_ref):
    a = jnp.dot(att_ref[...], wa_ref[...], preferred_element_type=F32)
    s = jnp.dot(ssm_ref[...], ws_ref[...], preferred_element_type=F32) * r_ref[:, 0:1]
    o_ref[...] = (ga_ref[...] * a + gs_ref[...] * s).astype(o_ref.dtype)


def _merge(att, ssm_in, ssm_r, wa, ws, gates, tm=512, tn=1024):
    m = att.shape[0]
    n = wa.shape[1]
    nj = n // tn
    return pl.pallas_call(
        _merge_kernel,
        grid=(nj, m // tm),
        in_specs=[
            pl.BlockSpec((tm, att.shape[1]), lambda j, i: (i, 0)),
            pl.BlockSpec((tm, ssm_in.shape[1]), lambda j, i: (i, 0)),
            pl.BlockSpec((tm, LANES), lambda j, i: (i, 0)),
            pl.BlockSpec((wa.shape[0], tn), lambda j, i: (0, j)),
            pl.BlockSpec((ws.shape[0], tn), lambda j, i: (0, j)),
            pl.BlockSpec((tm, tn), lambda j, i: (i, j)),
            pl.BlockSpec((tm, tn), lambda j, i: (i, nj + j)),
        ],
        out_specs=pl.BlockSpec((tm, tn), lambda j, i: (i, j)),
        out_shape=jax.ShapeDtypeStruct((m, n), BF16),
        compiler_params=_cparams("parallel", "parallel"),
        name="branch_merge",
    )(att, ssm_in, ssm_r, wa, ws, gates, gates)


def _out_kernel(m_ref, w_ref, x_ref, nw_ref, o_ref):
    y = x_ref[...] + jnp.dot(m_ref[...], w_ref[...], preferred_element_type=F32)
    y = y * lax.rsqrt(jnp.mean(y * y, axis=-1, keepdims=True) + NORM_EPS)
    o_ref[...] = y * nw_ref[...]


def _out_proj(merged, w, x, nw_row, tm=512):
    m, n = x.shape
    return pl.pallas_call(
        _out_kernel,
        grid=(m // tm,),
        in_specs=[
            pl.BlockSpec((tm, merged.shape[1]), lambda i: (i, 0)),
            pl.BlockSpec(w.shape, lambda i: (0, 0)),
            pl.BlockSpec((tm, n), lambda i: (i, 0)),
            pl.BlockSpec((1, n), lambda i: (0, 0)),
        ],
        out_specs=pl.BlockSpec((tm, n), lambda i: (i, 0)),
        out_shape=jax.ShapeDtypeStruct((m, n), F32),
        compiler_params=_cparams("parallel"),
        name="out_proj_norm",
    )(merged, w, x, nw_row)


def kernel(x_prompt, x_sample, norm_w, w_in, b_gate, conv_w, conv_b, dt_bias, a_log, d_skip, ssm_norm_w, w_attn_o, w_ssm_o, w_out, final_norm_w):
    d_model = x_prompt.shape[-1]
    d_inner = ssm_norm_w.shape[-1]
    conv_ch = conv_w.shape[-1]
    ssm_heads = d_skip.shape[-1]
    offs = np.cumsum([0, QKV_W, QKV_W, QKV_W, ATT_W, d_inner, conv_ch, 2 * ssm_heads, N_BRANCH * d_model])
    w = w_in[0].astype(BF16)
    w_dt = w[:, offs[6]:offs[7]]
    dt_b = dt_bias[0].astype(F32).reshape(1, -1)
    w_z, w_xbc, w_gates = w[:, offs[4]:offs[5]], w[:, offs[5]:offs[6]], w[:, offs[7]:offs[8]]
    wide_tile = 2048
    zero_b = jnp.zeros((1, conv_ch), F32)
    wa, ws, wo = w_attn_o[0].astype(BF16), w_ssm_o[0].astype(BF16), w_out[0].astype(BF16)
    a_log_row = a_log[0].astype(F32).reshape(1, -1)
    d_row = jnp.repeat(d_skip[0].astype(F32), SSM_HEAD_DIM).reshape(1, -1)
    conv_b = conv_b[0].reshape(1, -1)
    assert offs[3] % ATT_W == 0

    outs = []
    for x in (x_prompt, x_sample):
        b, l, _ = x.shape
        hn, hn4, hn16 = _prologue(x, norm_w[0])
        hn = hn.reshape(b * l, d_model)
        qkv = [_proj(h_g.reshape(b * l, d_model), w, zero_b, mode="qkv", tn=ATT_W, n=3 * ATT_W, col0=g,
                     col_step=N_GROUPS, out_dtype=BF16, tm=2048) for g, h_g in enumerate((hn, hn4, hn16))]
        gatt = _proj(hn, w, zero_b, mode="silu", tn=ATT_W, n=ATT_W, col0=offs[3] // ATT_W, out_dtype=F32)
        zs = _proj(hn, w_z, zero_b, mode="silu", tn=wide_tile, n=d_inner, out_dtype=F32)
        xbc = _proj_conv(hn, w_xbc, conv_w[0], conv_b, l, tn=1024).reshape(b, l, conv_ch)
        dt = _proj(hn, w_dt, dt_b, mode="softplus", tn=2 * ssm_heads, n=2 * ssm_heads, out_dtype=F32)
        gates = _proj(hn, w_gates, b_gate[0].reshape(1, -1), mode="sigmoid", tn=wide_tile, n=N_BRANCH * d_model,
                      out_dtype=F32)
        att = _attention(qkv[0], qkv[1], qkv[2], gatt, b, l).reshape(b * l, ATT_W)
        dt = dt.reshape(b, l, 2 * ssm_heads)
        yf = _ssd_pass(xbc, dt, a_log_row, None, b, l, d_inner, backward=False)
        post = (yf, zs.reshape(b, l, d_inner), d_row, ssm_norm_w[0].reshape(1, -1))
        ssm_in, ssm_r = _ssd_pass(xbc, dt, a_log_row, post, b, l, d_inner, backward=True)
        merged = _merge(att, ssm_in.reshape(b * l, d_inner), ssm_r.reshape(b * l, LANES), wa, ws, gates)
        out = _out_proj(merged, wo, x.reshape(b * l, d_model), final_norm_w.reshape(1, -1))
        outs.append(out.reshape(b, l, d_model))
    return tuple(outs)
```

```python
import functools
import math

import numpy as np
import jax
import jax.numpy as jnp
from jax import lax
from jax.experimental import pallas as pl
from jax.experimental.pallas import tpu as pltpu

F32 = jnp.float32
BF16 = jnp.bfloat16

NORM_EPS = 1e-5
ATT_GROUPS = ((128, 1), (512, 4), (2048, 16))
N_GROUPS = 3
HEADS_PER_GROUP = 12
HEAD_DIM = 128
ATT_HEADS = N_GROUPS * HEADS_PER_GROUP
ATT_W = HEADS_PER_GROUP * HEAD_DIM
QKV_W = ATT_HEADS * HEAD_DIM
ALIBI_MAX_EXP = 8.0
HALF = 64
SSM_HEAD_DIM = 64
SSM_GROUPS = 8
D_STATE = 128
D_CONV = 5
N_BRANCH = 2

VMEM_LIMIT_BYTES = 56 * 1024 * 1024
LANES = 128
PERM_BLOCK = 256
ATT_BLOCK = 2048
SUB = 128
NEG_BIG = -1e30
LOG2E = math.log2(math.e)


def _cparams(*sem):
    return pltpu.CompilerParams(dimension_semantics=sem, vmem_limit_bytes=VMEM_LIMIT_BYTES)


def _perm_matrix(d):
    n = PERM_BLOCK // d
    p = np.zeros((PERM_BLOCK, PERM_BLOCK), np.float32)
    for r in range(d):
        for i in range(n):
            p[r * n + i, i * d + r] = 1.0
    return jnp.asarray(p, BF16)


def _prologue_kernel(x_ref, w_ref, p4_ref, p16_ref, nat_ref, o4_ref, o16_ref):
    x = x_ref[...]
    y = x * lax.rsqrt(jnp.mean(x * x, axis=-1, keepdims=True) + NORM_EPS)
    yb = (y * w_ref[...]).astype(BF16)
    nat_ref[...] = yb
    y4 = jnp.dot(p4_ref[...], yb, preferred_element_type=F32).astype(BF16)
    n4 = PERM_BLOCK // 4
    for r in range(4):
        o4_ref[r] = y4[r * n4:(r + 1) * n4]
    y16 = jnp.dot(p16_ref[...], yb, preferred_element_type=F32).astype(BF16)
    n16 = PERM_BLOCK // 16
    for r in range(16):
        o16_ref[r] = y16[r * n16:(r + 1) * n16]


def _prologue(x, norm_w):
    b, l, d = x.shape
    nblk = l // PERM_BLOCK
    return pl.pallas_call(
        _prologue_kernel,
        grid=(b, nblk),
        in_specs=[
            pl.BlockSpec((None, PERM_BLOCK, d), lambda bi, i: (bi, i, 0)),
            pl.BlockSpec((1, d), lambda bi, i: (0, 0)),
            pl.BlockSpec((PERM_BLOCK, PERM_BLOCK), lambda bi, i: (0, 0)),
            pl.BlockSpec((PERM_BLOCK, PERM_BLOCK), lambda bi, i: (0, 0)),
        ],
        out_specs=[
            pl.BlockSpec((None, PERM_BLOCK, d), lambda bi, i: (bi, i, 0)),
            pl.BlockSpec((None, 4, PERM_BLOCK // 4, d), lambda bi, i: (bi, 0, i, 0)),
            pl.BlockSpec((None, 16, PERM_BLOCK // 16, d), lambda bi, i: (bi, 0, i, 0)),
        ],
        out_shape=[
            jax.ShapeDtypeStruct((b, l, d), BF16),
            jax.ShapeDtypeStruct((b, 4, l // 4, d), BF16),
            jax.ShapeDtypeStruct((b, 16, l // 16, d), BF16),
        ],
        compiler_params=_cparams("parallel", "parallel"),
        name="prologue_norm_permute",
    )(x, norm_w.reshape(1, d), _perm_matrix(4), _perm_matrix(16))


def _proj_kernel(x_ref, w_ref, b_ref, o_ref, *, mode):
    acc = jnp.dot(x_ref[...], w_ref[...], preferred_element_type=F32)
    if mode == "qkv":
        scale = jnp.where(pl.program_id(0) == 0, HEAD_DIM ** -0.5 * LOG2E, 1.0).astype(F32)
        acc = acc * scale
    elif mode == "silu":
        acc = acc * jax.nn.sigmoid(acc)
    elif mode == "softplus":
        acc = jax.nn.softplus(acc + b_ref[...])
    elif mode == "sigmoid":
        acc = jax.nn.sigmoid(acc + b_ref[...])
    o_ref[...] = acc.astype(o_ref.dtype)


def _proj(x, w, bias, *, mode, tn, n, out_dtype, col0=0, col_step=1, tm=1024):
    m, k = x.shape
    tm = min(tm, m)
    nj, ni = n // tn, m // tm
    if mode == "qkv":
        out_shape = jax.ShapeDtypeStruct((nj, m, tn), out_dtype)
        out_spec = pl.BlockSpec((None, tm, tn), lambda j, i: (j, i, 0))
    else:
        out_shape = jax.ShapeDtypeStruct((m, n), out_dtype)
        out_spec = pl.BlockSpec((tm, tn), lambda j, i: (i, j))
    return pl.pallas_call(
        functools.partial(_proj_kernel, mode=mode),
        grid=(nj, ni),
        in_specs=[
            pl.BlockSpec((tm, k), lambda j, i: (i, 0)),
            pl.BlockSpec((k, tn), lambda j, i: (0, col0 + col_step * j)),
            pl.BlockSpec((1, tn), lambda j, i: (0, j)),
        ],
        out_specs=out_spec,
        out_shape=out_shape,
        compiler_params=_cparams("parallel", "parallel"),
        name="in_proj_" + mode,
    )(x, w, bias)


CONV_HALO = 16
CONV_CHUNK = 256


def _proj_conv_kernel(xp_ref, xc_ref, xn_ref, w_ref, cw_ref, cb_ref, o_ref, *, tiles_per_seq):
    tm = xc_ref.shape[0]
    i = pl.program_id(1)
    pos = lax.rem(i, tiles_per_seq)
    lhs = jnp.concatenate([xp_ref[...], xc_ref[...], xn_ref[...]], axis=0)
    acc = jnp.dot(lhs, w_ref[...], preferred_element_type=F32)
    keep_lo = jnp.where(pos == 0, 0.0, 1.0)
    keep_hi = jnp.where(pos == tiles_per_seq - 1, 0.0, 1.0)
    w = cw_ref[...]
    half = D_CONV // 2
    pad = 8
    n_ext = CONV_CHUNK + 2 * pad
    for c in range(tm // CONV_CHUNK):
        lo = CONV_HALO + c * CONV_CHUNK - pad
        ext = acc[lo:lo + n_ext]
        if c == 0:
            ext = jnp.concatenate([ext[:pad] * keep_lo, ext[pad:]], axis=0)
        if c == tm // CONV_CHUNK - 1:
            ext = jnp.concatenate([ext[:n_ext - pad], ext[n_ext - pad:] * keep_hi], axis=0)
        out = cb_ref[...]
        for k in range(D_CONV):
            shift = (half - k) % n_ext
            tap = ext if shift == 0 else pltpu.roll(ext, shift, axis=0)
            out = out + tap[pad:pad + CONV_CHUNK] * w[k:k + 1]
        o_ref[c * CONV_CHUNK:(c + 1) * CONV_CHUNK, :] = (out * jax.nn.sigmoid(out)).astype(o_ref.dtype)


def _proj_conv(x, w, conv_w, conv_b, l, *, tn, tm=1024):
    m, k = x.shape
    n = w.shape[1]
    hb = tm // CONV_HALO
    last = m // CONV_HALO - 1
    return pl.pallas_call(
        functools.partial(_proj_conv_kernel, tiles_per_seq=l // tm),
        grid=(n // tn, m // tm),
        in_specs=[
            pl.BlockSpec((CONV_HALO, k), lambda j, i: (jnp.maximum(i * hb - 1, 0), 0)),
            pl.BlockSpec((tm, k), lambda j, i: (i, 0)),
            pl.BlockSpec((CONV_HALO, k), lambda j, i: (jnp.minimum((i + 1) * hb, last), 0)),
            pl.BlockSpec((k, tn), lambda j, i: (0, j)),
            pl.BlockSpec((D_CONV, tn), lambda j, i: (0, j)),
            pl.BlockSpec((1, tn), lambda j, i: (0, j)),
        ],
        out_specs=pl.BlockSpec((tm, tn), lambda j, i: (i, j)),
        out_shape=jax.ShapeDtypeStruct((m, n), BF16),
        compiler_params=_cparams("parallel", "parallel"),
        name="in_proj_conv",
    )(x, x, x, w, conv_w, conv_b)


def _attn_kernel(sl_ref, rel_ref,
                 q0, k0p, k0c, k0n, v0p, v0c, v0n,
                 q1, k1p, k1c, k1n, v1p, v1c, v1n,
                 q2, k2p, k2c, k2n, v2p, v2c, v2n,
                 gate_ref, o_ref, acc_s, lse_s):
    i = pl.program_id(1)
    h = pl.program_id(2)
    is_first = i == 0
    is_last = i == pl.num_programs(1) - 1
    ones = jnp.ones((2 * SUB, HEAD_DIM), BF16)

    def window(p_ref, c_ref, n_ref, lead, j, nsub):
        lo, hi = j * SUB - HALF, (j + 1) * SUB + HALF
        parts = []
        if j == 0:
            parts.append(p_ref[lead + (slice(None), slice(None))])
            lo = 0
        own_hi = min(hi, nsub * SUB)
        parts.append(c_ref[lead + (pl.ds(lo, own_hi - lo), slice(None))])
        if j == nsub - 1:
            parts.append(n_ref[lead + (slice(None), slice(None))])
        return parts[0] if len(parts) == 1 else jnp.concatenate(parts, axis=0)

    absrel = rel_ref[...]
    col = lax.broadcasted_iota(jnp.int32, (SUB, 2 * SUB), 1)
    in_window = absrel <= float(HALF)
    mask_lo = jnp.logical_and(is_first, col < HALF)
    mask_hi = jnp.logical_and(is_last, col >= HALF + SUB)

    def bias(g):
        dist = absrel * float(ATT_GROUPS[g][1])
        return jnp.where(in_window, -(sl_ref[g, h] * dist) * LOG2E, NEG_BIG)

    def attend(g, q, k, v, nb, rows):
        s = lax.dot_general(q, k, (((1,), (1,)), ((), ())), preferred_element_type=F32) + nb
        m = jnp.max(s, axis=-1, keepdims=True)
        p = jnp.exp2(s - m).astype(BF16)
        al = jnp.dot(p, jnp.concatenate([v, ones], axis=1), preferred_element_type=F32)
        lsum = al[:, HEAD_DIM:]
        acc_s[g, rows, :] = al[:, :HEAD_DIM] / lsum
        lse_s[g, rows, :] = m + jnp.log2(lsum)

    def group(g, q_ref, kp, kc, kn, vp, vc, vn, lead, r, nsub, js=None):
        d = ATT_GROUPS[g][1]
        nb = nbs[g]
        for j in (range(nsub) if js is None else js):
            nbj = nb
            if j == 0:
                nbj = jnp.where(mask_lo, NEG_BIG, nbj)
            if j == nsub - 1:
                nbj = jnp.where(mask_hi, NEG_BIG, nbj)
            q = q_ref[lead + (pl.ds(j * SUB, SUB), slice(None))]
            rows = pl.ds(j * SUB, SUB) if d == 1 else pl.ds(j * SUB * d + r, SUB, stride=d)
            attend(g, q, window(kp, kc, kn, lead, j, nsub), window(vp, vc, vn, lead, j, nsub), nbj, rows)

    nbs = [bias(g) for g in range(N_GROUPS)]
    chunk = 256

    def merge(cix):
        rows = pl.ds(cix * chunk, chunk)
        m0, m1, m2 = lse_s[0, rows, :], lse_s[1, rows, :], lse_s[2, rows, :]
        mm = jnp.maximum(jnp.maximum(m0, m1), m2)
        w0, w1, w2 = jnp.exp2(m0 - mm), jnp.exp2(m1 - mm), jnp.exp2(m2 - mm)
        num = w0 * acc_s[0, rows, :] + w1 * acc_s[1, rows, :] + w2 * acc_s[2, rows, :]
        o_ref[rows, :] = (num / (w0 + w1 + w2) * gate_ref[rows, :]).astype(o_ref.dtype)

    for r in range(16):
        group(2, q2, k2p, k2c, k2n, v2p, v2c, v2n, (r,), r, 1)
    n0, n1 = ATT_BLOCK // SUB, ATT_BLOCK // 4 // SUB
    for half_ix in range(2):
        group(0, q0, k0p, k0c, k0n, v0p, v0c, v0n, (), 0, n0, range(half_ix * n0 // 2, (half_ix + 1) * n0 // 2))
        for r in range(4):
            group(1, q1, k1p, k1c, k1n, v1p, v1c, v1n, (r,), r, n1, range(half_ix * n1 // 2, (half_ix + 1) * n1 // 2))
        for cix in range(half_ix * ATT_BLOCK // chunk // 2, (half_ix + 1) * ATT_BLOCK // chunk // 2):
            merge(cix)


def _alibi_table():
    hh = np.arange(1, ATT_HEADS + 1, dtype=np.float32)
    return jnp.exp2(-ALIBI_MAX_EXP * jnp.asarray(hh) / ATT_HEADS).reshape(N_GROUPS, HEADS_PER_GROUP)


def _rel_table():
    qi = np.arange(SUB)[:, None]
    kj = np.arange(2 * SUB)[None, :] - HALF
    return jnp.asarray(np.abs(qi - kj), F32)


def _attention(qkv0, qkv1, qkv2, gate, b, l):
    nblk = l // ATT_BLOCK
    qkv0 = qkv0.reshape(3, b, l, ATT_W)
    qkv1 = qkv1.reshape(3, b, 4, l // 4, ATT_W)
    qkv2 = qkv2.reshape(3, b, 16, l // 16, ATT_W)
    gate = gate.reshape(b, l, ATT_W)

    def specs(d):
        own = ATT_BLOCK // d
        hb = own // HALF
        nh = l // d // HALF
        lead = (None, None) if d == 1 else (None, None, d)
        zero = () if d == 1 else (0,)

        def mk(rows, which, fn):
            return pl.BlockSpec(lead + (rows, HEAD_DIM), lambda bi, i, h: (which, bi) + zero + (fn(i), h))

        cur = lambda i: i
        prv = lambda i: jnp.maximum(i * hb - 1, 0)
        nxt = lambda i: jnp.minimum((i + 1) * hb, nh - 1)
        return [mk(own, 0, cur),
                mk(HALF, 1, prv), mk(own, 1, cur), mk(HALF, 1, nxt),
                mk(HALF, 2, prv), mk(own, 2, cur), mk(HALF, 2, nxt)]

    in_specs = [pl.BlockSpec(memory_space=pltpu.SMEM),
                pl.BlockSpec((SUB, 2 * SUB), lambda bi, i, h: (0, 0))]
    in_specs += specs(1) + specs(4) + specs(16)
    in_specs += [pl.BlockSpec((None, ATT_BLOCK, HEAD_DIM), lambda bi, i, h: (bi, i, h))]
    return pl.pallas_call(
        _attn_kernel,
        grid=(b, nblk, HEADS_PER_GROUP),
        in_specs=in_specs,
        out_specs=pl.BlockSpec((None, ATT_BLOCK, HEAD_DIM), lambda bi, i, h: (bi, i, h)),
        out_shape=jax.ShapeDtypeStruct((b, l, ATT_W), BF16),
        scratch_shapes=[pltpu.VMEM((N_GROUPS, ATT_BLOCK, HEAD_DIM), F32)] * 2,
        compiler_params=_cparams("parallel", "parallel", "arbitrary"),
        name="dilated_attention",
    )(_alibi_table(), _rel_table(), *([qkv0] * 7), *([qkv1] * 7), *([qkv2] * 7), gate)


SSD_Q = 128
SSD_BLOCK = 256


def _bf16_terms(v, n):
    out = []
    for _ in range(n):
        t = v.astype(BF16).astype(F32)
        out.append(t)
        v = v - t
    return out


def _prefix_sum_rows(x):
    n = x.shape[0]
    row = lax.broadcasted_iota(jnp.int32, x.shape, 0)
    k = 1
    while k < n:
        x = x + jnp.where(row >= k, pltpu.roll(x, k, axis=0), 0.0)
        k *= 2
    return x


def _ssd_kernel(*refs, heads, groups, backward):
    if backward:
        x_ref, b_ref, c_ref, dt_ref, a_ref, e_ref, yf_ref, zs_ref, d_ref, w_ref, o_ref, r_ref, st_ref = refs
    else:
        x_ref, b_ref, c_ref, dt_ref, a_ref, e_ref, o_ref, st_ref = refs
    q = SSD_Q
    p = SSM_HEAD_DIM
    n = D_STATE
    hpg = heads // groups
    nc = x_ref.shape[0] // q

    @pl.when(pl.program_id(1) == 0)
    def _():
        st_ref[...] = jnp.zeros(st_ref.shape, F32)

    row = lax.broadcasted_iota(jnp.int32, (q, q), 0)
    col = lax.broadcasted_iota(jnp.int32, (q, q), 1)
    lower, upper = row >= col, row <= col
    is_fwd_lane = lax.broadcasted_iota(jnp.int32, (q, 2 * heads), 1) < heads
    a_all = -jnp.exp(a_ref[...])

    def split(v):
        return jnp.concatenate(_bf16_terms(v, 2), axis=1).astype(BF16)

    log2e = math.log2(math.e)
    lane2 = lax.broadcasted_iota(jnp.int32, (q, 2 * p), 1)

    def chunk(c, carry):
        cc = nc - 1 - c if backward else c
        rows = pl.ds(pl.multiple_of(cc * q, q), q)
        dt = dt_ref[rows, :]
        da = dt * a_all
        pre = _prefix_sum_rows(da)
        tot = pre[q - 1:q, :]
        acs_all = jnp.where(is_fwd_lane, pre, tot - pre + da)
        wx_all = jnp.dot(split(dt * jnp.exp(tot - acs_all)), e_ref[...],
                         preferred_element_type=F32)
        sc_all = jnp.dot(split(jnp.exp(acs_all)), e_ref[...],
                         preferred_element_type=F32)
        ssq = jnp.zeros((q, 1), F32)
        if not backward:
            as2 = acs_all * log2e
            bs2 = (acs_all - jnp.log(dt)) * log2e
            bs_t = bs2.T
            cbs = [lax.dot_general(c_ref[rows, g * n:(g + 1) * n], b_ref[rows, g * n:(g + 1) * n],
                                   (((1,), (1,)), ((), ())), preferred_element_type=F32) for g in range(groups)]
        for g in range(groups):
            cols = slice(g * hpg * p, (g + 1) * hpg * p)
            bg = b_ref[rows, g * n:(g + 1) * n]
            cg = c_ref[rows, g * n:(g + 1) * n]
            xg = x_ref[rows, cols]
            sc = sc_all[:, cols]
            wx = wx_all[:, cols]
            cdx = sc[0:1, :] if backward else sc[q - 1:q, :]
            st = st_ref[g]
            yg = jnp.dot(cg, st.astype(BF16), preferred_element_type=F32) * sc
            if not backward:
                for hh in range(0, hpg, 2):
                    ls = []
                    for h in (g * hpg + hh, g * hpg + hh + 1):
                        hb = heads + h
                        ef = jnp.exp2(as2[:, h:h + 1] - bs_t[h:h + 1, :])
                        eb = jnp.exp2(as2[:, hb:hb + 1] - bs_t[hb:hb + 1, :])
                        dec = jnp.where(lower, ef, 0.0) + jnp.where(upper, eb, 0.0)
                        ls.append((dec * cbs[g]).astype(BF16))
                    pair = xg[:, hh * p:(hh + 2) * p]
                    zero = jnp.zeros_like(pair)
                    rhs = jnp.concatenate([jnp.where(lane2 < p, pair, zero), jnp.where(lane2 >= p, pair, zero)],
                                          axis=0)
                    yd = jnp.dot(jnp.concatenate(ls, axis=1), rhs, preferred_element_type=F32)
                    pc = slice(hh * p, (hh + 2) * p)
                    o_ref[rows, g * hpg * p + hh * p:g * hpg * p + (hh + 2) * p] = yg[:, pc] + yd
            xf = xg.astype(F32)
            if backward:
                yg = (yg + yf_ref[rows, cols] + d_ref[:, cols] * xf) * zs_ref[rows, cols]
                ssq = ssq + jnp.sum(yg * yg, axis=-1, keepdims=True)
                o_ref[rows, cols] = (yg * w_ref[:, cols]).astype(o_ref.dtype)
            xdec = (xf * wx).astype(BF16)
            new = lax.dot_general(bg, xdec, (((0,), (0,)), ((), ())), preferred_element_type=F32)
            st_ref[g] = st * cdx + new
        if backward:
            inv = lax.rsqrt(ssq * (1.0 / (heads * p)) + NORM_EPS)
            r_ref[rows, :] = jnp.broadcast_to(inv, (q, r_ref.shape[-1]))
        return carry

    lax.fori_loop(0, nc, chunk, 0)


def _ssd_pass(xbc, dt, a_log_row, post, b, l, hp, backward):
    gn = SSM_GROUPS * D_STATE
    assert hp % gn == 0
    heads = hp // SSM_HEAD_DIM
    nblk = l // SSD_BLOCK
    e = np.zeros((4 * heads, hp), np.float32)
    off = heads if backward else 0
    for h in range(heads):
        e[off + h, h * SSM_HEAD_DIM:(h + 1) * SSM_HEAD_DIM] = 1.0
        e[2 * heads + off + h, h * SSM_HEAD_DIM:(h + 1) * SSM_HEAD_DIM] = 1.0
    const = lambda bi, i: (0, 0)

    def tok(w, cblk=0):
        return pl.BlockSpec((None, SSD_BLOCK, w), lambda bi, i: (bi, nblk - 1 - i if backward else i, cblk))

    in_specs = [tok(hp), tok(gn, hp // gn), tok(gn, hp // gn + 1), tok(2 * heads),
                pl.BlockSpec((1, 2 * heads), const), pl.BlockSpec((4 * heads, hp), const)]
    args = [xbc, xbc, xbc, dt, a_log_row, jnp.asarray(e, BF16)]
    scratch = [pltpu.VMEM((SSM_GROUPS, D_STATE, hp // SSM_GROUPS), F32)]
    if backward:
        in_specs += [tok(hp), tok(hp), pl.BlockSpec((1, hp), const), pl.BlockSpec((1, hp), const)]
        args += list(post)
        out_specs = [tok(hp), tok(LANES)]
        out_shape = [jax.ShapeDtypeStruct((b, l, hp), BF16), jax.ShapeDtypeStruct((b, l, LANES), F32)]
    else:
        out_specs = tok(hp)
        out_shape = jax.ShapeDtypeStruct((b, l, hp), F32)
    return pl.pallas_call(
        functools.partial(_ssd_kernel, heads=heads, groups=SSM_GROUPS, backward=backward),
        grid=(b, nblk),
        in_specs=in_specs,
        out_specs=out_specs,
        out_shape=out_shape,
        scratch_shapes=scratch,
        compiler_params=_cparams("parallel", "arbitrary"),
        name="ssd_scan_bwd" if backward else "ssd_scan_fwd",
    )(*args)


def _merge_kernel(att_ref, ssm_ref, r_ref, wa_ref, ws_ref, ga_ref, gs_ref, o_ref):
    a = jnp.dot(att_ref[...], wa_ref[...], preferred_element_type=F32)
    s = jnp.dot(ssm_ref[...], ws_ref[...], preferred_element_type=F32) * r_ref[:, 0:1]
    o_ref[...] = (ga_ref[...] * a + gs_ref[...] * s).astype(o_ref.dtype)


def _merge(att, ssm_in, ssm_r, wa, ws, gates, tm=512, tn=1024):
    m = att.shape[0]
    n = wa.shape[1]
    nj = n // tn
    return pl.pallas_call(
        _merge_kernel,
        grid=(nj, m // tm),
        in_specs=[
            pl.BlockSpec((tm, att.shape[1]), lambda j, i: (i, 0)),
            pl.BlockSpec((tm, ssm_in.shape[1]), lambda j, i: (i, 0)),
            pl.BlockSpec((tm, LANES), lambda j, i: (i, 0)),
            pl.BlockSpec((wa.shape[0], tn), lambda j, i: (0, j)),
            pl.BlockSpec((ws.shape[0], tn), lambda j, i: (0, j)),
            pl.BlockSpec((tm, tn), lambda j, i: (i, j)),
            pl.BlockSpec((tm, tn), lambda j, i: (i, nj + j)),
        ],
        out_specs=pl.BlockSpec((tm, tn), lambda j, i: (i, j)),
        out_shape=jax.ShapeDtypeStruct((m, n), BF16),
        compiler_params=_cparams("parallel", "parallel"),
        name="branch_merge",
    )(att, ssm_in, ssm_r, wa, ws, gates, gates)


def _out_kernel(m_ref, w_ref, x_ref, nw_ref, o_ref):
    y = x_ref[...] + jnp.dot(m_ref[...], w_ref[...], preferred_element_type=F32)
    y = y * lax.rsqrt(jnp.mean(y * y, axis=-1, keepdims=True) + NORM_EPS)
    o_ref[...] = y * nw_ref[...]


def _out_proj(merged, w, x, nw_row, tm=512):
    m, n = x.shape
    return pl.pallas_call(
        _out_kernel,
        grid=(m // tm,),
        in_specs=[
            pl.BlockSpec((tm, merged.shape[1]), lambda i: (i, 0)),
            pl.BlockSpec(w.shape, lambda i: (0, 0)),
            pl.BlockSpec((tm, n), lambda i: (i, 0)),
            pl.BlockSpec((1, n), lambda i: (0, 0)),
        ],
        out_specs=pl.BlockSpec((tm, n), lambda i: (i, 0)),
        out_shape=jax.ShapeDtypeStruct((m, n), F32),
        compiler_params=_cparams("parallel"),
        name="out_proj_norm",
    )(merged, w, x, nw_row)


def kernel(x_prompt, x_sample, norm_w, w_in, b_gate, conv_w, conv_b, dt_bias, a_log, d_skip, ssm_norm_w, w_attn_o, w_ssm_o, w_out, final_norm_w):
    d_model = x_prompt.shape[-1]
    d_inner = ssm_norm_w.shape[-1]
    conv_ch = conv_w.shape[-1]
    ssm_heads = d_skip.shape[-1]
    offs = np.cumsum([0, QKV_W, QKV_W, QKV_W, ATT_W, d_inner, conv_ch, 2 * ssm_heads, N_BRANCH * d_model])
    w = w_in[0].astype(BF16)
    w_dt = w[:, offs[6]:offs[7]]
    dt_b = dt_bias[0].astype(F32).reshape(1, -1)
    w_z, w_xbc, w_gates = w[:, offs[4]:offs[5]], w[:, offs[5]:offs[6]], w[:, offs[7]:offs[8]]
    wide_tile = 2048
    zero_b = jnp.zeros((1, conv_ch), F32)
    wa, ws, wo = w_attn_o[0].astype(BF16), w_ssm_o[0].astype(BF16), w_out[0].astype(BF16)
    a_log_row = a_log[0].astype(F32).reshape(1, -1)
    d_row = jnp.repeat(d_skip[0].astype(F32), SSM_HEAD_DIM).reshape(1, -1)
    conv_b = conv_b[0].reshape(1, -1)
    assert offs[3] % ATT_W == 0

    outs = []
    for x in (x_prompt, x_sample):
        b, l, _ = x.shape
        hn, hn4, hn16 = _prologue(x, norm_w[0])
        hn = hn.reshape(b * l, d_model)
        qkv = [_proj(h_g.reshape(b * l, d_model), w, zero_b, mode="qkv", tn=ATT_W, n=3 * ATT_W, col0=g,
                     col_step=N_GROUPS, out_dtype=BF16, tm=2048) for g, h_g in enumerate((hn, hn4, hn16))]
        gatt = _proj(hn, w, zero_b, mode="silu", tn=ATT_W, n=ATT_W, col0=offs[3] // ATT_W, out_dtype=F32)
        zs = _proj(hn, w_z, zero_b, mode="silu", tn=wide_tile, n=d_inner, out_dtype=F32)
        xbc = _proj_conv(hn, w_xbc, conv_w[0], conv_b, l, tn=1024).reshape(b, l, conv_ch)
        dt = _proj(hn, w_dt, dt_b, mode="softplus", tn=2 * ssm_heads, n=2 * ssm_heads, out_dtype=F32)
        gates = _proj(hn, w_gates, b_gate[0].reshape(1, -1), mode="sigmoid", tn=wide_tile, n=N_BRANCH * d_model,
                      out_dtype=F32)
        att = _attention(qkv[0], qkv[1], qkv[2], gatt, b, l).reshape(b * l, ATT_W)
        dt = dt.reshape(b, l, 2 * ssm_heads)
        yf = _ssd_pass(xbc, dt, a_log_row, None, b, l, d_inner, backward=False)
        post = (yf, zs.reshape(b, l, d_inner), d_row, ssm_norm_w[0].reshape(1, -1))
        ssm_in, ssm_r = _ssd_pass(xbc, dt, a_log_row, post, b, l, d_inner, backward=True)
        merged = _merge(att, ssm_in.reshape(b * l, d_inner), ssm_r.reshape(b * l, LANES), wa, ws, gates)
        out = _out_proj(merged, wo, x.reshape(b * l, d_model), final_norm_w.reshape(1, -1))
        outs.append(out.reshape(b, l, d_model))
    return tuple(outs)
```
